```python
import math
import jax
import jax.numpy as jnp
from jax import lax
import numpy as np

D_MODEL = 1024
BATCH = 8
SEQ = 4096
DEPTH = 2

N_META = 16
CONV_K = 4
RMS_EPS = 1e-6
L2_EPS = 1e-6
D_FF = 4 * D_MODEL
N_BRANCH = 3

GDN_HEADS = 8
GDN_DK = 128
GDN_DV = 128
GDN_CHUNK = 64
GDN_QK_W = GDN_HEADS * GDN_DK
GDN_V_W = GDN_HEADS * GDN_DV

SSD_HEADS = 16
SSD_HEAD_DIM = 64
SSD_INNER = SSD_HEADS * SSD_HEAD_DIM
SSD_GROUPS = 4
SSD_HPG = SSD_HEADS // SSD_GROUPS
SSD_STATE = 128
SSD_CHUNK = 128
SSD_CONV_DIM = SSD_INNER + 2 * SSD_GROUPS * SSD_STATE

SWA_Q_HEADS = 16
SWA_KV_HEADS = 4
SWA_REP = SWA_Q_HEADS // SWA_KV_HEADS
SWA_HEAD_DIM = 64
SWA_WINDOW = 128
SWA_Q_W = SWA_Q_HEADS * SWA_HEAD_DIM
SWA_KV_W = SWA_KV_HEADS * SWA_HEAD_DIM

IN_SIZES = (GDN_QK_W, GDN_QK_W, GDN_V_W, GDN_V_W, GDN_HEADS, GDN_HEADS,
            SSD_INNER, SSD_CONV_DIM, SSD_HEADS,
            SWA_Q_W, SWA_KV_W, SWA_KV_W,
            N_BRANCH * D_MODEL)
IN_W = sum(IN_SIZES)

kernel_name = "hybrid_gdn_ssd_swa_sink_block"


def rmsnorm(x, w):
    xf = x.astype(jnp.float32)
    y = xf * lax.rsqrt(jnp.mean(xf * xf, axis=-1, keepdims=True) + RMS_EPS)
    return (y * w.astype(jnp.float32)).astype(x.dtype)


def l2norm(x):
    return x * lax.rsqrt(jnp.sum(x * x, axis=-1, keepdims=True) + L2_EPS)


def split_in(u):
    offs = np.cumsum(np.array(IN_SIZES))[:-1].tolist()
    return jnp.split(u, offs, axis=-1)


def causal_dwconv(x, w, b=None):
    y = lax.conv_general_dilated(
        x, w[:, None, :].astype(x.dtype), window_strides=(1,), padding=[(CONV_K - 1, 0)],
        dimension_numbers=("NWC", "WIO", "NWC"), feature_group_count=x.shape[-1])
    if b is not None:
        y = y + b.astype(x.dtype)
    return y


def pad_front(t, pad):
    return jnp.pad(t, [(0, 0), (pad, 0)] + [(0, 0)] * (t.ndim - 2))


def softmax_with_sink(s, sink):
    m = jnp.maximum(jnp.max(s, axis=-1, keepdims=True), sink)
    e = jnp.exp(s - m)
    return e / (jnp.sum(e, axis=-1, keepdims=True) + jnp.exp(sink - m))


def gated_delta_chunked(q, k, v, g, beta):
    bsz, t_len, nh, dk = q.shape
    dv = v.shape[-1]
    c = GDN_CHUNK
    nc = t_len // c
    q = q.reshape(bsz, nc, c, nh, dk).transpose(0, 3, 1, 2, 4) * (dk ** -0.5)
    k = k.reshape(bsz, nc, c, nh, dk).transpose(0, 3, 1, 2, 4)
    v = v.reshape(bsz, nc, c, nh, dv).transpose(0, 3, 1, 2, 4)
    g = g.reshape(bsz, nc, c, nh).transpose(0, 3, 1, 2)
    beta = beta.reshape(bsz, nc, c, nh).transpose(0, 3, 1, 2)
    gam = jnp.cumsum(g, axis=-1)
    tri_incl = jnp.tril(jnp.ones((c, c), dtype=bool))
    tri_strict = jnp.tril(jnp.ones((c, c), dtype=bool), -1)
    decay = jnp.exp(jnp.where(tri_incl, gam[..., :, None] - gam[..., None, :], -jnp.inf))
    kb = k * beta[..., None]
    a_low = jnp.where(tri_strict, jnp.einsum("bhncd,bhnsd->bhncs", kb, k) * decay, 0.0)
    eye = jnp.eye(c, dtype=jnp.float32)
    rhs = jnp.concatenate([v * beta[..., None], kb * jnp.exp(gam)[..., None]], axis=-1)
    sol = lax.linalg.triangular_solve(a_low + eye, rhs, left_side=True, lower=True, unit_diagonal=True)
    u = sol[..., :dv]
    w = sol[..., dv:]
    attn_qk = jnp.einsum("bhncd,bhnsd->bhncs", q, k) * decay
    q_dec = q * jnp.exp(gam)[..., None]
    g_last = gam[..., -1]
    k_tail = k * jnp.exp(g_last[..., None] - gam)[..., None]

    def step(s_state, inp):
        u_c, w_c, qd_c, a_c, kt_c, gl_c = inp
        v_new = u_c - jnp.einsum("bhcd,bhde->bhce", w_c, s_state)
        o_c = jnp.einsum("bhcd,bhde->bhce", qd_c, s_state) + jnp.einsum("bhcs,bhse->bhce", a_c, v_new)
        s_state = s_state * jnp.exp(gl_c)[..., None, None] + jnp.einsum("bhcd,bhce->bhde", kt_c, v_new)
        return s_state, o_c

    xs = (jnp.moveaxis(u, 2, 0), jnp.moveaxis(w, 2, 0), jnp.moveaxis(q_dec, 2, 0),
          jnp.moveaxis(attn_qk, 2, 0), jnp.moveaxis(k_tail, 2, 0), jnp.moveaxis(g_last, -1, 0))
    s0 = jnp.zeros((bsz, nh, dk, dv), jnp.float32)
    _, o = lax.scan(step, s0, xs)
    return o.transpose(1, 0, 3, 2, 4).reshape(bsz, t_len, nh, dv)


def gdn_branch(q, k, v, gate, b, a, conv_w, a_log, dt_bias, norm_w):
    dtype = q.dtype
    bsz, seq_len = q.shape[:2]
    qkv = jax.nn.silu(causal_dwconv(jnp.concatenate([q, k, v], axis=-1), conv_w))
    q, k, v = jnp.split(qkv, [GDN_QK_W, 2 * GDN_QK_W], axis=-1)
    q = l2norm(q.astype(jnp.float32).reshape(bsz, seq_len, GDN_HEADS, GDN_DK))
    k = l2norm(k.astype(jnp.float32).reshape(bsz, seq_len, GDN_HEADS, GDN_DK))
    v = v.astype(jnp.float32).reshape(bsz, seq_len, GDN_HEADS, GDN_DV)
    beta = jax.nn.sigmoid(b.astype(jnp.float32))
    g = -jnp.exp(a_log.astype(jnp.float32)) * jax.nn.softplus(a.astype(jnp.float32) + dt_bias.astype(jnp.float32))
    pad = GDN_CHUNK - N_META
    o = gated_delta_chunked(pad_front(q, pad), pad_front(k, pad), pad_front(v, pad),
                            pad_front(g, pad), pad_front(beta, pad))[:, pad:]
    gate = gate.astype(jnp.float32).reshape(bsz, seq_len, GDN_HEADS, GDN_DV)
    o = rmsnorm(o, norm_w) * jax.nn.silu(gate)
    return o.reshape(bsz, seq_len, GDN_V_W).astype(dtype)


def ssd_chunked(xdt, adt, bm, cm):
    bsz, t_len, ng, nr, hp = xdt.shape
    c = SSD_CHUNK
    nc = t_len // c
    xdt = xdt.reshape(bsz, nc, c, ng, nr, hp)
    bm = bm.reshape(bsz, nc, c, ng, -1)
    cm = cm.reshape(bsz, nc, c, ng, -1)
    acum = jnp.cumsum(adt.reshape(bsz, nc, c, ng, nr).transpose(0, 3, 4, 1, 2), axis=-1)
    tri = jnp.tril(jnp.ones((c, c), dtype=bool))
    lmat = jnp.exp(jnp.where(tri, acum[..., :, None] - acum[..., None, :], -jnp.inf))
    cb = jnp.einsum("bclgn,bcsgn->bgcls", cm, bm)
    y_diag = jnp.einsum("bgcls,bgrcls,bcsgrp->bclgrp", cb, lmat, xdt)
    decay_states = jnp.exp(acum[..., -1:] - acum)
    states = jnp.einsum("bclgn,bgrcl,bclgrp->bcgrpn", bm, decay_states, xdt)
    chunk_decay = jnp.exp(acum[..., -1])

    def step(h, inp):
        st, dec = inp
        return h * dec[..., None, None] + st, h

    h0 = jnp.zeros((bsz, ng, nr, hp, states.shape[-1]), jnp.float32)
    _, h_in = lax.scan(step, h0, (jnp.moveaxis(states, 1, 0), jnp.moveaxis(chunk_decay, -1, 0)))
    h_in = jnp.moveaxis(h_in, 0, 1)
    y_off = jnp.einsum("bclgn,bcgrpn,bgrcl->bclgrp", cm, h_in, jnp.exp(acum))
    return (y_diag + y_off).reshape(bsz, t_len, ng, nr, hp)


def ssd_branch(z, xbc, dt, conv_w, conv_b, dt_bias, a_log, d_skip, norm_w):
    dtype = z.dtype
    bsz, seq_len = z.shape[:2]
    xbc = jax.nn.silu(causal_dwconv(xbc, conv_w, conv_b)).astype(jnp.float32)
    xs, bm, cm = jnp.split(xbc, [SSD_INNER, SSD_INNER + SSD_GROUPS * SSD_STATE], axis=-1)
    xs = xs.reshape(bsz, seq_len, SSD_GROUPS, SSD_HPG, SSD_HEAD_DIM)
    bm = bm.reshape(bsz, seq_len, SSD_GROUPS, SSD_STATE)
    cm = cm.reshape(bsz, seq_len, SSD_GROUPS, SSD_STATE)
    dtp = jax.nn.softplus(dt.astype(jnp.float32) + dt_bias.astype(jnp.float32))
    dtp = dtp.reshape(bsz, seq_len, SSD_GROUPS, SSD_HPG)
    a = -jnp.exp(a_log.astype(jnp.float32)).reshape(SSD_GROUPS, SSD_HPG)
    pad = SSD_CHUNK - N_META
    y = ssd_chunked(pad_front(xs * dtp[..., None], pad), pad_front(dtp * a, pad),
                    pad_front(bm, pad), pad_front(cm, pad))[:, pad:]
    y = y + d_skip.astype(jnp.float32).reshape(SSD_GROUPS, SSD_HPG)[:, :, None] * xs
    y = y.reshape(bsz, seq_len, SSD_INNER) * jax.nn.silu(z.astype(jnp.float32))
    y = y.reshape(bsz, seq_len, SSD_GROUPS, SSD_INNER // SSD_GROUPS)
    y = y * lax.rsqrt(jnp.mean(y * y, axis=-1, keepdims=True) + RMS_EPS)
    y = y * norm_w.astype(jnp.float32).reshape(SSD_GROUPS, SSD_INNER // SSD_GROUPS)
    return y.reshape(bsz, seq_len, SSD_INNER).astype(dtype)


def swa_branch(q, k, v, sinks):
    dtype = q.dtype
    bsz, seq_len = q.shape[:2]
    s_len = seq_len - N_META
    wdw = SWA_WINDOW
    nb = s_len // wdw
    q = q.astype(jnp.float32).reshape(bsz, seq_len, SWA_KV_HEADS, SWA_REP, SWA_HEAD_DIM) * (SWA_HEAD_DIM ** -0.5)
    k = k.astype(jnp.float32).reshape(bsz, seq_len, SWA_KV_HEADS, SWA_HEAD_DIM)
    v = v.astype(jnp.float32).reshape(bsz, seq_len, SWA_KV_HEADS, SWA_HEAD_DIM)
    sink = sinks.astype(jnp.float32).reshape(SWA_KV_HEADS, SWA_REP)
    qm, km, vm = q[:, :N_META], k[:, :N_META], v[:, :N_META]
    s_m = jnp.einsum("bqhrd,bkhd->bhrqk", qm, km)
    s_m = jnp.where(jnp.tril(jnp.ones((N_META, N_META), dtype=bool)), s_m, -jnp.inf)
    p_m = softmax_with_sink(s_m, sink[None, :, :, None, None])
    o_m = jnp.einsum("bhrqk,bkhd->bqhrd", p_m, vm).reshape(bsz, N_META, SWA_Q_W)
    qr = q[:, N_META:].reshape(bsz, nb, wdw, SWA_KV_HEADS, SWA_REP, SWA_HEAD_DIM)
    kr = k[:, N_META:].reshape(bsz, nb, wdw, SWA_KV_HEADS, SWA_HEAD_DIM)
    vr = v[:, N_META:].reshape(bsz, nb, wdw, SWA_KV_HEADS, SWA_HEAD_DIM)
    zpad = [(0, 0), (1, 0), (0, 0), (0, 0), (0, 0)]
    kband = jnp.concatenate([jnp.pad(kr, zpad)[:, :-1], kr], axis=2)
    vband = jnp.concatenate([jnp.pad(vr, zpad)[:, :-1], vr], axis=2)
    qpos = jnp.arange(wdw)[:, None] + wdw
    kpos = jnp.arange(2 * wdw)[None, :]
    band = (kpos <= qpos) & (kpos > qpos - wdw)
    prev_ok = (jnp.arange(nb)[:, None] > 0) | (jnp.arange(2 * wdw)[None, :] >= wdw)
    mask = band[None, :, :] & prev_ok[:, None, :]
    s_loc = jnp.einsum("bnqhrd,bnkhd->bnhrqk", qr, kband)
    s_loc = jnp.where(mask[None, :, None, None, :, :], s_loc, -jnp.inf)
    s_meta = jnp.einsum("bnqhrd,bkhd->bnhrqk", qr, km)
    p = softmax_with_sink(jnp.concatenate([s_loc, s_meta], axis=-1), sink[None, None, :, :, None, None])
    o_r = (jnp.einsum("bnhrqk,bnkhd->bnqhrd", p[..., :2 * wdw], vband)
           + jnp.einsum("bnhrqk,bkhd->bnqhrd", p[..., 2 * wdw:], vm))
    o_r = o_r.reshape(bsz, s_len, SWA_Q_W)
    return jnp.concatenate([o_m, o_r], axis=1).astype(dtype)


def setup_inputs(seed: int = 0) -> dict:
    key = jax.random.key(seed)
    ks = jax.random.split(key, 23)
    f32 = jnp.float32

    def nrm(k, shape, scale):
        return jax.random.normal(k, shape, f32) * scale

    def gain(k, shape):
        return 1.0 + 0.02 * jax.random.normal(k, shape, f32)

    def dt_bias(k, shape):
        dt = jnp.exp(jax.random.uniform(k, shape, f32, math.log(1e-3), math.log(1e-1)))
        return dt + jnp.log(-jnp.expm1(-dt))

    def a_log(k, shape):
        return jnp.log(jax.random.uniform(k, shape, f32, 1.0, 16.0))

    return {
        "x": nrm(ks[0], (BATCH, SEQ, D_MODEL), 1.0),
        "meta_tokens": nrm(ks[1], (N_META, D_MODEL), 1.0),
        "norm1_w": gain(ks[2], (DEPTH, D_MODEL)),
        "w_in": nrm(ks[3], (DEPTH, D_MODEL, IN_W), D_MODEL ** -0.5),
        "gdn_conv_w": nrm(ks[4], (DEPTH, CONV_K, 2 * GDN_QK_W + GDN_V_W), CONV_K ** -0.5),
        "gdn_a_log": a_log(ks[5], (DEPTH, GDN_HEADS)),
        "gdn_dt_bias": dt_bias(ks[6], (DEPTH, GDN_HEADS)),
        "gdn_norm_w": gain(ks[7], (DEPTH, GDN_DV)),
        "ssd_conv_w": nrm(ks[8], (DEPTH, CONV_K, SSD_CONV_DIM), CONV_K ** -0.5),
        "ssd_conv_b": nrm(ks[9], (DEPTH, SSD_CONV_DIM), 0.02),
        "ssd_dt_bias": dt_bias(ks[10], (DEPTH, SSD_HEADS)),
        "ssd_a_log": a_log(ks[11], (DEPTH, SSD_HEADS)),
        "ssd_d": 1.0 + 0.1 * jax.random.normal(ks[12], (DEPTH, SSD_HEADS), f32),
        "ssd_norm_w": gain(ks[13], (DEPTH, SSD_INNER)),
        "swa_sinks": nrm(ks[14], (DEPTH, SWA_Q_HEADS), 0.5),
        "w_proj_gdn": nrm(ks[15], (DEPTH, GDN_V_W, D_MODEL), GDN_V_W ** -0.5),
        "w_proj_ssd": nrm(ks[16], (DEPTH, SSD_INNER, D_MODEL), SSD_INNER ** -0.5),
        "w_proj_swa": nrm(ks[17], (DEPTH, SWA_Q_W, D_MODEL), SWA_Q_W ** -0.5),
        "w_out": nrm(ks[18], (DEPTH, D_MODEL, D_MODEL), D_MODEL ** -0.5),
        "norm2_w": gain(ks[19], (DEPTH, D_MODEL)),
        "w_up": nrm(ks[20], (DEPTH, D_MODEL, D_FF), D_MODEL ** -0.5),
        "w_down": nrm(ks[21], (DEPTH, D_FF, D_MODEL), D_FF ** -0.5),
        "final_norm_w": gain(ks[22], (D_MODEL,)),
    }


def reference(x, meta_tokens, norm1_w, w_in, gdn_conv_w, gdn_a_log, gdn_dt_bias, gdn_norm_w,
              ssd_conv_w, ssd_conv_b, ssd_dt_bias, ssd_a_log, ssd_d, ssd_norm_w, swa_sinks,
              w_proj_gdn, w_proj_ssd, w_proj_swa, w_out, norm2_w, w_up, w_down, final_norm_w):
    bsz = x.shape[0]
    meta = jnp.broadcast_to(meta_tokens.astype(x.dtype)[None], (bsz, N_META, D_MODEL))
    h = jnp.concatenate([meta, x], axis=1)
    for l in range(DEPTH):
        u = rmsnorm(h, norm1_w[l]) @ w_in[l]
        (a_q, a_k, a_v, a_gate, a_b, a_a, b_z, b_xbc, b_dt, c_q, c_k, c_v, gate_logits) = split_in(u)
        y_gdn = gdn_branch(a_q, a_k, a_v, a_gate, a_b, a_a, gdn_conv_w[l], gdn_a_log[l], gdn_dt_bias[l], gdn_norm_w[l])
        y_ssd = ssd_branch(b_z, b_xbc, b_dt, ssd_conv_w[l], ssd_conv_b[l], ssd_dt_bias[l], ssd_a_log[l], ssd_d[l], ssd_norm_w[l])
        y_swa = swa_branch(c_q, c_k, c_v, swa_sinks[l])
        gates = jax.nn.sigmoid(gate_logits.astype(jnp.float32)).astype(h.dtype)
        g_a, g_b, g_c = jnp.split(gates, N_BRANCH, axis=-1)
        merged = (g_a * (y_gdn @ w_proj_gdn[l]) + g_b * (y_ssd @ w_proj_ssd[l])
                  + g_c * (y_swa @ w_proj_swa[l]))
        h = h + merged @ w_out[l]
        hn = rmsnorm(h, norm2_w[l])
        h = h + jnp.square(jax.nn.relu(hn @ w_up[l])) @ w_down[l]
    return rmsnorm(h, final_norm_w)[:, N_META:]
```

```python
import functools

import jax
import jax.numpy as jnp
from jax import lax
from jax.experimental import pallas as pl
from jax.experimental.pallas import tpu as pltpu

F32 = jnp.float32
BF16 = jnp.bfloat16

D_MODEL = 1024
D_FF = 4 * D_MODEL
N_META = 16
RMS_EPS = 1e-6
L2_EPS = 1e-6

BLK = 128
PAD = BLK - N_META

GDN_HEADS = 8
GDN_D = 128
SSD_HEADS = 16
SSD_P = 64
SSD_GROUPS = 4
SSD_N = 128
SWA_Q_HEADS = 16
SWA_KV_HEADS = 4
SWA_DH = 64

COL_GDN_Q, COL_GDN_K, COL_GDN_V, COL_GDN_GATE = 0, 1024, 2048, 3072
COL_SSD_Z, COL_SSD_X, COL_SSD_BC = 4096, 5120, 6144
COL_SWA_Q = 7168
COL_GATES = 8192
COL_SWA_K, COL_SWA_V = 11264, 11520
U_WIDTH = 11776
SMALL_W = 128
SMALL_ROWS = 32

VMEM_LIMIT = 52 * 1024 * 1024


def _cparams(sem):
    return pltpu.CompilerParams(dimension_semantics=sem, vmem_limit_bytes=VMEM_LIMIT)


def _pick_tile(n, candidates):
    for c in candidates:
        if n % c == 0:
            return c
    raise ValueError(f"no tile for {n}")


def _dot(a, b):
    return jnp.dot(a.astype(BF16), b.astype(BF16), preferred_element_type=F32)


def _dot_nt(a, b):
    return lax.dot_general(a.astype(BF16), b.astype(BF16), (((1,), (1,)), ((), ())), preferred_element_type=F32)


def _dot_tn(a, b):
    return lax.dot_general(a.astype(BF16), b.astype(BF16), (((0,), (0,)), ((), ())), preferred_element_type=F32)


def _dot_f32(a, b):
    return jnp.dot(a, b, preferred_element_type=F32, precision=lax.Precision.HIGHEST)


def _split_bf16(a):
    hi = a.astype(BF16)
    lo = (a - hi.astype(F32)).astype(BF16)
    return hi, lo


def _dot_x3(a, b):
    ah, al = _split_bf16(a)
    bh, bl = _split_bf16(b)
    d = functools.partial(jnp.dot, preferred_element_type=F32)
    return d(ah, bh) + d(ah, bl) + d(al, bh)


def _sigmoid(x):
    return 1.0 / (1.0 + jnp.exp(-x))


def _silu(x):
    return x * _sigmoid(x)


def _softplus(x):
    return jnp.maximum(x, 0.0) + jnp.log1p(jnp.exp(-jnp.abs(x)))


def _rmsnorm(x, w):
    return x * lax.rsqrt(jnp.mean(x * x, axis=-1, keepdims=True) + RMS_EPS) * w


def _causal_conv4(x, halo, w):
    c = x.shape[0]
    xe = jnp.concatenate([halo, x], axis=0)
    return xe[5:5 + c] * w[0:1] + xe[6:6 + c] * w[1:2] + xe[7:7 + c] * w[2:3] + x * w[3:4]


def _solve_unit_lower(a, x):
    x = x - _dot_x3(a, x)
    p = a
    for _ in range(6):
        p = _dot_x3(p, p)
        x = x + _dot_x3(p, x)
    return x


def _inproj_kernel(h_ref, nw_ref, w_ref, ws_ref, u_ref, small_ref, smallt_ref, hn_ref):
    @pl.when(pl.program_id(1) == 0)
    def _():
        hn = _rmsnorm(h_ref[...], nw_ref[...]).astype(BF16)
        hn_ref[...] = hn
        small = jnp.dot(hn, ws_ref[...], preferred_element_type=F32)
        small_ref[...] = small
        smallt_ref[...] = small.T[:SMALL_ROWS]

    u_ref[...] = jnp.dot(hn_ref[...], w_ref[...], preferred_element_type=F32)


def _inproj(h, norm_w, w_main, w_small):
    t = h.shape[0]
    tm = _pick_tile(t, (1536, 1024, 768, 512, 384, 256, 128))
    tn = 512
    return pl.pallas_call(
        _inproj_kernel,
        grid=(t // tm, U_WIDTH // tn),
        in_specs=[
            pl.BlockSpec((tm, D_MODEL), lambda i, j: (i, 0)),
            pl.BlockSpec((1, D_MODEL), lambda i, j: (0, 0)),
            pl.BlockSpec((D_MODEL, tn), lambda i, j: (0, j)),
            pl.BlockSpec((D_MODEL, SMALL_W), lambda i, j: (0, 0)),
        ],
        out_specs=[
            pl.BlockSpec((tm, tn), lambda i, j: (i, j)),
            pl.BlockSpec((tm, SMALL_W), lambda i, j: (i, 0)),
            pl.BlockSpec((SMALL_ROWS, tm), lambda i, j: (0, i)),
        ],
        out_shape=[
            jax.ShapeDtypeStruct((t, U_WIDTH), F32),
            jax.ShapeDtypeStruct((t, SMALL_W), F32),
            jax.ShapeDtypeStruct((SMALL_ROWS, t), F32),
        ],
        scratch_shapes=[pltpu.VMEM((tm, D_MODEL), BF16)],
        compiler_params=_cparams(("parallel", "arbitrary")),
        name="inproj",
    )(h, norm_w, w_main, w_small)


def _gdn_kernel(q_ref, k_ref, v_ref, gate_ref, sm_ref, smt_ref, cw_ref, alog_l_ref, dtb_l_ref, alog_c_ref,
                dtb_c_ref, nw_ref, o_ref, halo_ref, s_ref):
    c = pl.program_id(1)

    @pl.when(c == 0)
    def _():
        halo_ref[...] = jnp.zeros_like(halo_ref)
        s_ref[...] = jnp.zeros_like(s_ref)

    w_all = 3 * GDN_HEADS * GDN_D
    cw = cw_ref[...]
    halo = halo_ref[...]
    qin, kin, vin = q_ref[...], k_ref[...], v_ref[...]
    n = GDN_HEADS * GDN_D
    qc = _silu(_causal_conv4(qin, halo[:, 0:n], cw[:, 0:n]))
    kc = _silu(_causal_conv4(kin, halo[:, n:2 * n], cw[:, n:2 * n]))
    vc = _silu(_causal_conv4(vin, halo[:, 2 * n:w_all], cw[:, 2 * n:w_all]))
    halo_ref[:, 0:n] = qin[BLK - 8:BLK]
    halo_ref[:, n:2 * n] = kin[BLK - 8:BLK]
    halo_ref[:, 2 * n:w_all] = vin[BLK - 8:BLK]

    ri = lax.broadcasted_iota(jnp.int32, (BLK, BLK), 0)
    ci = lax.broadcasted_iota(jnp.int32, (BLK, BLK), 1)
    incl = ri >= ci
    strict = ri > ci
    valid_rows = (c * BLK + ri) >= PAD
    sm = sm_ref[...]
    beta_all = jnp.where(valid_rows, _sigmoid(sm), 0.0)
    g_all = jnp.where(valid_rows, -jnp.exp(alog_l_ref[...]) * _softplus(sm + dtb_l_ref[...]), 0.0)
    gam_all = _dot_f32(incl.astype(F32), g_all)
    ci8 = lax.broadcasted_iota(jnp.int32, (GDN_HEADS, BLK), 1)
    g_rows = jnp.where((c * BLK + ci8) >= PAD,
                       -jnp.exp(alog_c_ref[...]) * _softplus(smt_ref[8:16, :] + dtb_c_ref[...]), 0.0)
    gam_rows = _dot_f32(g_rows, (ri <= ci).astype(F32))
    nw = nw_ref[...]

    for h in range(GDN_HEADS):
        sl = slice(h * GDN_D, (h + 1) * GDN_D)
        qh, kh, vh = qc[:, sl], kc[:, sl], vc[:, sl]
        qh = qh * lax.rsqrt(jnp.sum(qh * qh, axis=-1, keepdims=True) + L2_EPS) * (GDN_D ** -0.5)
        kh = kh * lax.rsqrt(jnp.sum(kh * kh, axis=-1, keepdims=True) + L2_EPS)
        beta = beta_all[:, h:h + 1]
        gam_c = gam_all[:, 8 + h:9 + h]
        gam_r = gam_rows[h:h + 1, :]
        dmat = jnp.exp(jnp.where(incl, gam_c - gam_r, -jnp.inf))
        kb = kh * beta
        a = jnp.where(strict, _dot_nt(kb, kh) * dmat, 0.0)
        egam = jnp.exp(gam_c)
        sol = _solve_unit_lower(a, jnp.concatenate([vh * beta, kb * egam], axis=1))
        u, w = sol[:, :GDN_D], sol[:, GDN_D:]
        aqk = _dot_nt(qh, kh) * dmat
        glast = gam_c[BLK - 1:BLK, :]
        ktail = kh * jnp.exp(glast - gam_c)
        s = s_ref[h]
        ws = _dot(jnp.concatenate([w, qh * egam], axis=0), s)
        vnew = u - ws[:BLK]
        o = ws[BLK:] + _dot(aqk, vnew)
        s_ref[h] = s * jnp.exp(glast) + _dot_tn(ktail, vnew)
        gt = gate_ref[:, sl]
        o_ref[:, sl] = (_rmsnorm(o, nw) * _silu(gt)).astype(o_ref.dtype)


def _gdn(u, small, small_t, bsz, nb, conv_w, a_log, dt_bias, norm_w):
    t = u.shape[0]
    n = GDN_HEADS * GDN_D

    def lane_vec(v, off):
        return jnp.zeros((1, SMALL_W), F32).at[0, off:off + v.shape[0]].set(v.astype(F32))

    def ublk(col):
        return pl.BlockSpec((BLK, n), lambda b, c: (b * nb + c, col // n))

    def full(shape):
        return pl.BlockSpec(shape, lambda b, c: (0,) * len(shape))

    return pl.pallas_call(
        _gdn_kernel,
        grid=(bsz, nb),
        in_specs=[
            ublk(COL_GDN_Q), ublk(COL_GDN_K), ublk(COL_GDN_V), ublk(COL_GDN_GATE),
            pl.BlockSpec((BLK, SMALL_W), lambda b, c: (b * nb + c, 0)),
            pl.BlockSpec((SMALL_ROWS, BLK), lambda b, c: (0, b * nb + c)),
            full((4, 3 * n)), full((1, SMALL_W)), full((1, SMALL_W)), full((GDN_HEADS, 1)), full((GDN_HEADS, 1)),
            full((1, GDN_D)),
        ],
        out_specs=pl.BlockSpec((BLK, n), lambda b, c: (b * nb + c, 0)),
        out_shape=jax.ShapeDtypeStruct((t, n), BF16),
        scratch_shapes=[pltpu.VMEM((8, 3 * n), F32), pltpu.VMEM((GDN_HEADS, GDN_D, GDN_D), F32)],
        compiler_params=_cparams(("parallel", "arbitrary")),
        name="gdn_mixer",
    )(u, u, u, u, small, small_t, conv_w.astype(F32), lane_vec(a_log, 8), lane_vec(dt_bias, 8),
      a_log.astype(F32).reshape(GDN_HEADS, 1), dt_bias.astype(F32).reshape(GDN_HEADS, 1),
      norm_w.astype(F32).reshape(1, GDN_D))


def _ssd_kernel(z_ref, x_ref, bc_ref, sm_ref, smt_ref, cw_ref, cb_ref, alog_l_ref, dtb_l_ref, alog_c_ref,
                dtb_c_ref, d_ref, nw_ref, e_ref, o_ref, halo_ref, st_ref):
    c = pl.program_id(1)

    @pl.when(c == 0)
    def _():
        halo_ref[...] = jnp.zeros_like(halo_ref)
        st_ref[...] = jnp.zeros_like(st_ref)

    n = SSD_HEADS * SSD_P
    gn = SSD_GROUPS * SSD_N
    gw = n // SSD_GROUPS
    cw = cw_ref[...]
    cb = cb_ref[...]
    halo = halo_ref[...]
    xin, bcin = x_ref[...], bc_ref[...]
    xs = _silu(_causal_conv4(xin, halo[:, 0:n], cw[:, 0:n]) + cb[:, 0:n])
    bcs = _silu(_causal_conv4(bcin, halo[:, n:2 * n], cw[:, n:2 * n]) + cb[:, n:2 * n])
    halo_ref[:, 0:n] = xin[BLK - 8:BLK]
    halo_ref[:, n:2 * n] = bcin[BLK - 8:BLK]
    bm, cm = bcs[:, 0:gn], bcs[:, gn:2 * gn]

    ri = lax.broadcasted_iota(jnp.int32, (BLK, BLK), 0)
    ci = lax.broadcasted_iota(jnp.int32, (BLK, BLK), 1)
    incl = ri >= ci
    valid_rows = (c * BLK + ri) >= PAD
    dtp_all = jnp.where(valid_rows, _softplus(sm_ref[...] + dtb_l_ref[...]), 0.0)
    acum_all = _dot_f32(incl.astype(F32), dtp_all * (-jnp.exp(alog_l_ref[...])))
    e = e_ref[...]
    dtp_x = _dot_f32(dtp_all, e)
    acum_x = _dot_f32(acum_all, e)
    ci16 = lax.broadcasted_iota(jnp.int32, (SSD_HEADS, BLK), 1)
    dtp_rows = jnp.where((c * BLK + ci16) >= PAD, _softplus(smt_ref[16:32, :] + dtb_c_ref[...]), 0.0)
    acum_rows = _dot_f32(dtp_rows * (-jnp.exp(alog_c_ref[...])), (ri <= ci).astype(F32))

    xdt = xs * dtp_x
    eacum_x = jnp.exp(acum_x)
    alast_x = acum_x[BLK - 1:BLK, :]
    xdt_dec = xdt * jnp.exp(alast_x - acum_x)
    valid_out = (c * BLK + lax.broadcasted_iota(jnp.int32, (BLK, gw), 0)) >= PAD
    hpg = SSD_HEADS // SSD_GROUPS

    for g in range(SSD_GROUPS):
        gs = slice(g * gw, (g + 1) * gw)
        bg = bm[:, g * SSD_N:(g + 1) * SSD_N]
        cg = cm[:, g * SSD_N:(g + 1) * SSD_N]
        cbm = _dot_nt(cg, bg)
        st = st_ref[:, gs]
        y = _dot(cg, st) * eacum_x[:, gs]
        parts = []
        for r in range(hpg):
            hh = g * hpg + r
            lm = jnp.exp(jnp.where(incl, acum_all[:, 16 + hh:17 + hh] - acum_rows[hh:hh + 1, :], -jnp.inf))
            parts.append(_dot(cbm * lm, xdt[:, hh * SSD_P:(hh + 1) * SSD_P]))
        y = y + jnp.concatenate(parts, axis=1)
        st_ref[:, gs] = st * jnp.exp(alast_x[:, gs]) + _dot_tn(bg, xdt_dec[:, gs])
        y = y + d_ref[:, gs] * xs[:, gs]
        y = y * _silu(z_ref[:, gs])
        y = _rmsnorm(y, nw_ref[:, gs])
        o_ref[:, gs] = jnp.where(valid_out, y, 0.0).astype(o_ref.dtype)


def _ssd(u, small, small_t, bsz, nb, conv_w, conv_b, dt_bias, a_log, d_skip, norm_w):
    t = u.shape[0]
    n = SSD_HEADS * SSD_P

    def lane_vec(v, off):
        return jnp.zeros((1, SMALL_W), F32).at[0, off:off + v.shape[0]].set(v.astype(F32))

    def ublk(col):
        return pl.BlockSpec((BLK, n), lambda b, c: (b * nb + c, col // n))

    def full(shape):
        return pl.BlockSpec(shape, lambda b, c: (0,) * len(shape))

    expand = jnp.zeros((SMALL_W, n), F32).at[16:16 + SSD_HEADS].set(
        jnp.repeat(jnp.eye(SSD_HEADS, dtype=F32), SSD_P, axis=1))
    return pl.pallas_call(
        _ssd_kernel,
        grid=(bsz, nb),
        in_specs=[
            ublk(COL_SSD_Z), ublk(COL_SSD_X), ublk(COL_SSD_BC),
            pl.BlockSpec((BLK, SMALL_W), lambda b, c: (b * nb + c, 0)),
            pl.BlockSpec((SMALL_ROWS, BLK), lambda b, c: (0, b * nb + c)),
            full((4, 2 * n)), full((1, 2 * n)), full((1, SMALL_W)), full((1, SMALL_W)), full((SSD_HEADS, 1)),
            full((SSD_HEADS, 1)), full((1, n)), full((1, n)), full((SMALL_W, n)),
        ],
        out_specs=pl.BlockSpec((BLK, n), lambda b, c: (b * nb + c, 0)),
        out_shape=jax.ShapeDtypeStruct((t, n), BF16),
        scratch_shapes=[pltpu.VMEM((8, 2 * n), F32), pltpu.VMEM((SSD_N, n), F32)],
        compiler_params=_cparams(("parallel", "arbitrary")),
        name="ssd_mixer",
    )(u, u, u, small, small_t, conv_w.astype(F32), conv_b.astype(F32).reshape(1, 2 * n),
      lane_vec(a_log, 16), lane_vec(dt_bias, 16), a_log.astype(F32).reshape(SSD_HEADS, 1),
      dt_bias.astype(F32).reshape(SSD_HEADS, 1), jnp.repeat(d_skip.astype(F32), SSD_P).reshape(1, n),
      norm_w.astype(F32).reshape(1, n), expand)


def _swa_kernel(sink_ref, q_ref, kc_ref, kp_ref, km_ref, vc_ref, vp_ref, vm_ref, o_ref):
    c = pl.program_id(1)
    ri = lax.broadcasted_iota(jnp.int32, (BLK, BLK), 0)
    ci = lax.broadcasted_iota(jnp.int32, (BLK, BLK), 1)
    cur_ok = (ci <= ri) & ((c > 0) | (ci >= PAD))
    prev_ok = ((c >= 2) & (ci > ri)) | ((c == 1) & (ci >= PAD))
    meta_ok = jnp.broadcast_to(c >= 2, (BLK, N_META))
    rep = SWA_Q_HEADS // SWA_KV_HEADS
    neg = -jnp.inf
    for hk in range(SWA_KV_HEADS):
        ks = slice(hk * SWA_DH, (hk + 1) * SWA_DH)
        kc, kp, km = kc_ref[:, ks], kp_ref[:, ks], km_ref[:, ks]
        vc, vp, vm = vc_ref[:, ks], vp_ref[:, ks], vm_ref[:, ks]
        for r in range(rep):
            hq = hk * rep + r
            qs = slice(hq * SWA_DH, (hq + 1) * SWA_DH)
            qh = q_ref[:, qs] * (SWA_DH ** -0.5)
            s_c = jnp.where(cur_ok, _dot_nt(qh, kc), neg)
            s_p = jnp.where(prev_ok, _dot_nt(qh, kp), neg)
            s_m = jnp.where(meta_ok, _dot_nt(qh, km), neg)
            sink = sink_ref[hq]
            m = jnp.maximum(jnp.maximum(jnp.max(s_c, axis=-1, keepdims=True), jnp.max(s_p, axis=-1, keepdims=True)),
                            jnp.maximum(jnp.max(s_m, axis=-1, keepdims=True), sink))
            e_c, e_p, e_m = jnp.exp(s_c - m), jnp.exp(s_p - m), jnp.exp(s_m - m)
            den = (jnp.sum(e_c, axis=-1, keepdims=True) + jnp.sum(e_p, axis=-1, keepdims=True)
                   + jnp.sum(e_m, axis=-1, keepdims=True) + jnp.exp(sink - m))
            inv = 1.0 / den
            o = _dot(e_c * inv, vc) + _dot(e_p * inv, vp) + _dot(e_m * inv, vm)
            o_ref[:, qs] = o.astype(o_ref.dtype)


def _swa(u, bsz, nb, sinks):
    t = u.shape[0]
    nq = SWA_Q_HEADS * SWA_DH
    nkv = SWA_KV_HEADS * SWA_DH
    meta_blk = PAD // N_META
    rows16 = (nb * BLK) // N_META

    def cur(col):
        return pl.BlockSpec((BLK, nkv), lambda b, c: (b * nb + c, col // nkv))

    def prev(col):
        return pl.BlockSpec((BLK, nkv), lambda b, c: (b * nb + jnp.maximum(c - 1, 0), col // nkv))

    def meta(col):
        return pl.BlockSpec((N_META, nkv), lambda b, c: (b * rows16 + meta_blk, col // nkv))

    return pl.pallas_call(
        _swa_kernel,
        grid=(bsz, nb),
        in_specs=[
            pl.BlockSpec(memory_space=pltpu.SMEM),
            pl.BlockSpec((BLK, nq), lambda b, c: (b * nb + c, COL_SWA_Q // nq)),
            cur(COL_SWA_K), prev(COL_SWA_K), meta(COL_SWA_K),
            cur(COL_SWA_V), prev(COL_SWA_V), meta(COL_SWA_V),
        ],
        out_specs=pl.BlockSpec((BLK, nq), lambda b, c: (b * nb + c, 0)),
        out_shape=jax.ShapeDtypeStruct((t, nq), BF16),
        compiler_params=_cparams(("parallel", "parallel")),
        name="swa_mixer",
    )(sinks.astype(F32), u, u, u, u, u, u, u)


def _merge_kernel(h_ref, yg_ref, ys_ref, yc_ref, ga_ref, gb_ref, gc_ref, wg_ref, ws_ref, wc_ref, wo_ref, o_ref):
    d = functools.partial(jnp.dot, preferred_element_type=F32)
    merged = (_sigmoid(ga_ref[...]) * d(yg_ref[...], wg_ref[...])
              + _sigmoid(gb_ref[...]) * d(ys_ref[...], ws_ref[...])
              + _sigmoid(gc_ref[...]) * d(yc_ref[...], wc_ref[...]))
    o_ref[...] = h_ref[...] + d(merged.astype(BF16), wo_ref[...])


def _merge(h, y_gdn, y_ssd, y_swa, u, w_gdn, w_ssd, w_swa, w_out):
    t = h.shape[0]
    tm = _pick_tile(t, (512, 384, 256, 128))
    row = pl.BlockSpec((tm, D_MODEL), lambda i: (i, 0))
    wfull = pl.BlockSpec((D_MODEL, D_MODEL), lambda i: (0, 0))

    def gate(k):
        return pl.BlockSpec((tm, D_MODEL), lambda i: (i, COL_GATES // D_MODEL + k))

    return pl.pallas_call(
        _merge_kernel,
        grid=(t // tm,),
        in_specs=[row, row, row, row, gate(0), gate(1), gate(2), wfull, wfull, wfull, wfull],
        out_specs=row,
        out_shape=jax.ShapeDtypeStruct((t, D_MODEL), F32),
        compiler_params=_cparams(("parallel",)),
        name="merge",
    )(h, y_gdn, y_ssd, y_swa, u, u, u, w_gdn, w_ssd, w_swa, w_out)


def _mlp_kernel(h_ref, nw_ref, wu_ref, wd_ref, fw_ref, o_ref, hn_ref, acc_ref, *, final_norm):
    j = pl.program_id(1)

    @pl.when(j == 0)
    def _():
        hn_ref[...] = _rmsnorm(h_ref[...], nw_ref[...]).astype(BF16)
        acc_ref[...] = jnp.zeros_like(acc_ref)

    up = jnp.maximum(jnp.dot(hn_ref[...], wu_ref[...], preferred_element_type=F32), 0.0)
    acc_ref[...] += jnp.dot((up * up).astype(BF16), wd_ref[...], preferred_element_type=F32)

    @pl.when(j == pl.num_programs(1) - 1)
    def _():
        r = h_ref[...] + acc_ref[...]
        if final_norm:
            r = _rmsnorm(r, fw_ref[...])
        o_ref[...] = r


def _mlp(h, norm_w, w_up, w_down, final_w, final_norm):
    t = h.shape[0]
    tm = _pick_tile(t, (1024, 768, 512, 384, 256, 128))
    tf = 1024
    return pl.pallas_call(
        functools.partial(_mlp_kernel, final_norm=final_norm),
        grid=(t // tm, D_FF // tf),
        in_specs=[
            pl.BlockSpec((tm, D_MODEL), lambda i, j: (i, 0)),
            pl.BlockSpec((1, D_MODEL), lambda i, j: (0, 0)),
            pl.BlockSpec((D_MODEL, tf), lambda i, j: (0, j)),
            pl.BlockSpec((tf, D_MODEL), lambda i, j: (j, 0)),
            pl.BlockSpec((1, D_MODEL), lambda i, j: (0, 0)),
        ],
        out_specs=pl.BlockSpec((tm, D_MODEL), lambda i, j: (i, 0)),
        out_shape=jax.ShapeDtypeStruct((t, D_MODEL), F32),
        scratch_shapes=[pltpu.VMEM((tm, D_MODEL), BF16), pltpu.VMEM((tm, D_MODEL), F32)],
        compiler_params=_cparams(("parallel", "arbitrary")),
        name="mlp",
    )(h, norm_w, w_up, w_down, final_w)


def _permute_w_in(w):
    a_small = 4 * GDN_HEADS * GDN_D
    b_z = a_small + 2 * GDN_HEADS
    b_dt = b_z + 3 * SSD_HEADS * SSD_P
    c_q = b_dt + SSD_HEADS
    c_k = c_q + SWA_Q_HEADS * SWA_DH
    c_v = c_k + SWA_KV_HEADS * SWA_DH
    gates = c_v + SWA_KV_HEADS * SWA_DH
    main = jnp.concatenate([w[:, :a_small], w[:, b_z:b_dt], w[:, c_q:c_k], w[:, gates:], w[:, c_k:c_v],
                            w[:, c_v:gates]], axis=1)
    small = jnp.concatenate([w[:, a_small:b_z], w[:, b_dt:c_q],
                             jnp.zeros((w.shape[0], SMALL_W - SMALL_ROWS), w.dtype)], axis=1)
    return main.astype(BF16), small.astype(BF16)


def kernel(x, meta_tokens, norm1_w, w_in, gdn_conv_w, gdn_a_log, gdn_dt_bias, gdn_norm_w, ssd_conv_w, ssd_conv_b,
           ssd_dt_bias, ssd_a_log, ssd_d, ssd_norm_w, swa_sinks, w_proj_gdn, w_proj_ssd, w_proj_swa, w_out, norm2_w,
           w_up, w_down, final_norm_w):
    bsz, seq, dm = x.shape
    depth = w_in.shape[0]
    assert dm == D_MODEL and seq % BLK == 0
    nb = seq // BLK + 1
    lp = nb * BLK
    meta = jnp.broadcast_to(meta_tokens.astype(x.dtype)[None], (bsz, N_META, dm))
    h = jnp.concatenate([jnp.zeros((bsz, PAD, dm), x.dtype), meta, x], axis=1).reshape(bsz * lp, dm)
    final_w = final_norm_w.astype(F32).reshape(1, dm)
    for l in range(depth):
        w_main, w_small = _permute_w_in(w_in[l])
        u, small, small_t = _inproj(h, norm1_w[l].astype(F32).reshape(1, dm), w_main, w_small)
        y_gdn = _gdn(u, small, small_t, bsz, nb, gdn_conv_w[l], gdn_a_log[l], gdn_dt_bias[l], gdn_norm_w[l])
        y_ssd = _ssd(u, small, small_t, bsz, nb, ssd_conv_w[l], ssd_conv_b[l], ssd_dt_bias[l], ssd_a_log[l],
                     ssd_d[l], ssd_norm_w[l])
        y_swa = _swa(u, bsz, nb, swa_sinks[l])
        h = _merge(h, y_gdn, y_ssd, y_swa, u, w_proj_gdn[l].astype(BF16), w_proj_ssd[l].astype(BF16),
                   w_proj_swa[l].astype(BF16), w_out[l].astype(BF16))
        h = _mlp(h, norm2_w[l].astype(F32).reshape(1, dm), w_up[l].astype(BF16), w_down[l].astype(BF16), final_w,
                 final_norm=(l == depth - 1))
    return h.reshape(bsz, lp, dm)[:, BLK:]
```

```python
import functools

import jax
import jax.numpy as jnp
from jax import lax
from jax.experimental import pallas as pl
from jax.experimental.pallas import tpu as pltpu

F32 = jnp.float32
BF16 = jnp.bfloat16

D_MODEL = 1024
D_FF = 4 * D_MODEL
N_META = 16
RMS_EPS = 1e-6
L2_EPS = 1e-6

BLK = 128
PAD = BLK - N_META

GDN_HEADS = 8
GDN_D = 128
SSD_HEADS = 16
SSD_P = 64
SSD_GROUPS = 4
SSD_N = 128
SWA_Q_HEADS = 16
SWA_KV_HEADS = 4
SWA_DH = 64

COL_GDN_Q, COL_GDN_K, COL_GDN_V, COL_GDN_GATE = 0, 1024, 2048, 3072
COL_SSD_Z, COL_SSD_X, COL_SSD_BC = 4096, 5120, 6144
COL_SWA_Q = 7168
COL_GATES = 8192
COL_SWA_K, COL_SWA_V = 11264, 11520
U_WIDTH = 11776
SMALL_W = 128
SMALL_ROWS = 32

VMEM_LIMIT = 52 * 1024 * 1024


def _cparams(sem):
    return pltpu.CompilerParams(dimension_semantics=sem, vmem_limit_bytes=VMEM_LIMIT)


def _pick_tile(n, candidates):
    for c in candidates:
        if n % c == 0:
            return c
    raise ValueError(f"no tile for {n}")


def _dot(a, b):
    return jnp.dot(a.astype(BF16), b.astype(BF16), preferred_element_type=F32)


def _dot_nt(a, b):
    return lax.dot_general(a.astype(BF16), b.astype(BF16), (((1,), (1,)), ((), ())), preferred_element_type=F32)


def _dot_tn(a, b):
    return lax.dot_general(a.astype(BF16), b.astype(BF16), (((0,), (0,)), ((), ())), preferred_element_type=F32)


def _dot_f32(a, b):
    return jnp.dot(a, b, preferred_element_type=F32, precision=lax.Precision.HIGHEST)


def _split_bf16(a):
    hi = a.astype(BF16)
    lo = (a - hi.astype(F32)).astype(BF16)
    return hi, lo


def _dot_x3(a, b):
    ah, al = _split_bf16(a)
    bh, bl = _split_bf16(b)
    d = functools.partial(jnp.dot, preferred_element_type=F32)
    return d(ah, bh) + d(ah, bl) + d(al, bh)


def _sigmoid(x):
    return 1.0 / (1.0 + jnp.exp(-x))


def _silu(x):
    return x * _sigmoid(x)


def _softplus(x):
    return jnp.maximum(x, 0.0) + jnp.log1p(jnp.exp(-jnp.abs(x)))


def _rmsnorm(x, w):
    return x * lax.rsqrt(jnp.mean(x * x, axis=-1, keepdims=True) + RMS_EPS) * w


def _causal_conv4(x, halo, w):
    xe = jnp.concatenate([halo, x], axis=0)
    acc = x * w[3:4]
    for j in range(3):
        acc = acc + pltpu.roll(xe, 3 - j, axis=0)[8:] * w[j:j + 1]
    return acc


def _solve_unit_lower(a_list, x_list):
    xs = [x - _dot_x3(a, x) for a, x in zip(a_list, x_list)]
    ps = a_list
    for _ in range(6):
        ps = [_dot_x3(p, p) for p in ps]
        xs = [x + _dot_x3(p, x) for p, x in zip(ps, xs)]
    return xs


def _inproj_kernel(h_ref, nw_ref, w_ref, ws_ref, u_ref, small_ref, smallt_ref, hn_ref):
    @pl.when(pl.program_id(1) == 0)
    def _():
        hn = _rmsnorm(h_ref[...], nw_ref[...]).astype(BF16)
        hn_ref[...] = hn
        small = jnp.dot(hn, ws_ref[...], preferred_element_type=F32)
        small_ref[...] = small
        smallt_ref[...] = small.T[:SMALL_ROWS]

    u_ref[...] = jnp.dot(hn_ref[...], w_ref[...], preferred_element_type=F32)


def _inproj(h, norm_w, w_main, w_small):
    t = h.shape[0]
    tm = _pick_tile(t, (1536, 1024, 768, 512, 384, 256, 128))
    tn = 512
    return pl.pallas_call(
        _inproj_kernel,
        grid=(t // tm, U_WIDTH // tn),
        in_specs=[
            pl.BlockSpec((tm, D_MODEL), lambda i, j: (i, 0)),
            pl.BlockSpec((1, D_MODEL), lambda i, j: (0, 0)),
            pl.BlockSpec((D_MODEL, tn), lambda i, j: (0, j)),
            pl.BlockSpec((D_MODEL, SMALL_W), lambda i, j: (0, 0)),
        ],
        out_specs=[
            pl.BlockSpec((tm, tn), lambda i, j: (i, j)),
            pl.BlockSpec((tm, SMALL_W), lambda i, j: (i, 0)),
            pl.BlockSpec((SMALL_ROWS, tm), lambda i, j: (0, i)),
        ],
        out_shape=[
            jax.ShapeDtypeStruct((t, U_WIDTH), F32),
            jax.ShapeDtypeStruct((t, SMALL_W), F32),
            jax.ShapeDtypeStruct((SMALL_ROWS, t), F32),
        ],
        scratch_shapes=[pltpu.VMEM((tm, D_MODEL), BF16)],
        compiler_params=_cparams(("parallel", "arbitrary")),
        name="inproj",
    )(h, norm_w, w_main, w_small)


def _gdn_kernel(q_ref, k_ref, v_ref, gate_ref, sm_ref, smt_ref, cw_ref, alog_l_ref, dtb_l_ref, alog_c_ref,
                dtb_c_ref, nw_ref, o_ref, halo_ref, s_ref):
    c = pl.program_id(1)

    @pl.when(c == 0)
    def _():
        halo_ref[...] = jnp.zeros_like(halo_ref)
        s_ref[...] = jnp.zeros_like(s_ref)

    w_all = 3 * GDN_HEADS * GDN_D
    cw = cw_ref[...]
    halo = halo_ref[...]
    qin, kin, vin = q_ref[...], k_ref[...], v_ref[...]
    n = GDN_HEADS * GDN_D
    qc = _silu(_causal_conv4(qin, halo[:, 0:n], cw[:, 0:n]))
    kc = _silu(_causal_conv4(kin, halo[:, n:2 * n], cw[:, n:2 * n]))
    vc = _silu(_causal_conv4(vin, halo[:, 2 * n:w_all], cw[:, 2 * n:w_all]))
    halo_ref[:, 0:n] = qin[BLK - 8:BLK]
    halo_ref[:, n:2 * n] = kin[BLK - 8:BLK]
    halo_ref[:, 2 * n:w_all] = vin[BLK - 8:BLK]

    ri = lax.broadcasted_iota(jnp.int32, (BLK, BLK), 0)
    ci = lax.broadcasted_iota(jnp.int32, (BLK, BLK), 1)
    incl = ri >= ci
    strict = ri > ci
    valid_rows = (c * BLK + ri) >= PAD
    sm = sm_ref[...]
    beta_all = jnp.where(valid_rows, _sigmoid(sm), 0.0)
    g_all = jnp.where(valid_rows, -jnp.exp(alog_l_ref[...]) * _softplus(sm + dtb_l_ref[...]), 0.0)
    gam_all = _dot_f32(incl.astype(F32), g_all)
    ci8 = lax.broadcasted_iota(jnp.int32, (GDN_HEADS, BLK), 1)
    g_rows = jnp.where((c * BLK + ci8) >= PAD,
                       -jnp.exp(alog_c_ref[...]) * _softplus(smt_ref[8:16, :] + dtb_c_ref[...]), 0.0)
    gam_rows = _dot_f32(g_rows, (ri <= ci).astype(F32))
    nw = nw_ref[...]

    heads = range(GDN_HEADS)
    sls = [slice(h * GDN_D, (h + 1) * GDN_D) for h in heads]
    qs = [qc[:, sl] for sl in sls]
    ks = [kc[:, sl] for sl in sls]
    qs = [q * lax.rsqrt(jnp.sum(q * q, axis=-1, keepdims=True) + L2_EPS) * (GDN_D ** -0.5) for q in qs]
    ks = [k * lax.rsqrt(jnp.sum(k * k, axis=-1, keepdims=True) + L2_EPS) for k in ks]
    betas = [beta_all[:, h:h + 1] for h in heads]
    gam_cs = [gam_all[:, 8 + h:9 + h] for h in heads]
    dmats = [jnp.exp(jnp.where(incl, gam_cs[h] - gam_rows[h:h + 1, :], -jnp.inf)) for h in heads]
    kbs = [ks[h] * betas[h] for h in heads]
    egams = [jnp.exp(g) for g in gam_cs]
    a_list = [jnp.where(strict, _dot_nt(kbs[h], ks[h]) * dmats[h], 0.0) for h in heads]
    rhs = [jnp.concatenate([vc[:, sls[h]] * betas[h], kbs[h] * egams[h]], axis=1) for h in heads]
    sols = _solve_unit_lower(a_list, rhs)
    aqks = [_dot_nt(qs[h], ks[h]) * dmats[h] for h in heads]
    glasts = [g[BLK - 1:BLK, :] for g in gam_cs]
    ktails = [ks[h] * jnp.exp(glasts[h] - gam_cs[h]) for h in heads]
    states = [s_ref[h] for h in heads]
    wss = [_dot(jnp.concatenate([sols[h][:, GDN_D:], qs[h] * egams[h]], axis=0), states[h]) for h in heads]
    vnews = [sols[h][:, :GDN_D] - wss[h][:BLK] for h in heads]
    outs = [wss[h][BLK:] + _dot(aqks[h], vnews[h]) for h in heads]
    for h in heads:
        s_ref[h] = states[h] * jnp.exp(glasts[h]) + _dot_tn(ktails[h], vnews[h])
    for h in heads:
        gt = gate_ref[:, sls[h]]
        o_ref[:, sls[h]] = (_rmsnorm(outs[h], nw) * _silu(gt)).astype(o_ref.dtype)


def _gdn(u, small, small_t, bsz, nb, conv_w, a_log, dt_bias, norm_w):
    t = u.shape[0]
    n = GDN_HEADS * GDN_D

    def lane_vec(v, off):
        return jnp.zeros((1, SMALL_W), F32).at[0, off:off + v.shape[0]].set(v.astype(F32))

    def ublk(col):
        return pl.BlockSpec((BLK, n), lambda b, c: (b * nb + c, col // n))

    def full(shape):
        return pl.BlockSpec(shape, lambda b, c: (0,) * len(shape))

    return pl.pallas_call(
        _gdn_kernel,
        grid=(bsz, nb),
        in_specs=[
            ublk(COL_GDN_Q), ublk(COL_GDN_K), ublk(COL_GDN_V), ublk(COL_GDN_GATE),
            pl.BlockSpec((BLK, SMALL_W), lambda b, c: (b * nb + c, 0)),
            pl.BlockSpec((SMALL_ROWS, BLK), lambda b, c: (0, b * nb + c)),
            full((4, 3 * n)), full((1, SMALL_W)), full((1, SMALL_W)), full((GDN_HEADS, 1)), full((GDN_HEADS, 1)),
            full((1, GDN_D)),
        ],
        out_specs=pl.BlockSpec((BLK, n), lambda b, c: (b * nb + c, 0)),
        out_shape=jax.ShapeDtypeStruct((t, n), BF16),
        scratch_shapes=[pltpu.VMEM((8, 3 * n), F32), pltpu.VMEM((GDN_HEADS, GDN_D, GDN_D), F32)],
        compiler_params=_cparams(("parallel", "arbitrary")),
        name="gdn_mixer",
    )(u, u, u, u, small, small_t, conv_w.astype(F32), lane_vec(a_log, 8), lane_vec(dt_bias, 8),
      a_log.astype(F32).reshape(GDN_HEADS, 1), dt_bias.astype(F32).reshape(GDN_HEADS, 1),
      norm_w.astype(F32).reshape(1, GDN_D))


def _ssd_kernel(z_ref, x_ref, bc_ref, sm_ref, smt_ref, cw_ref, cb_ref, alog_l_ref, dtb_l_ref, alog_c_ref,
                dtb_c_ref, d_ref, nw_ref, e_ref, o_ref, halo_ref, st_ref):
    c = pl.program_id(1)

    @pl.when(c == 0)
    def _():
        halo_ref[...] = jnp.zeros_like(halo_ref)
        st_ref[...] = jnp.zeros_like(st_ref)

    n = SSD_HEADS * SSD_P
    gn = SSD_GROUPS * SSD_N
    gw = n // SSD_GROUPS
    cw = cw_ref[...]
    cb = cb_ref[...]
    halo = halo_ref[...]
    xin, bcin = x_ref[...], bc_ref[...]
    xs = _silu(_causal_conv4(xin, halo[:, 0:n], cw[:, 0:n]) + cb[:, 0:n])
    bcs = _silu(_causal_conv4(bcin, halo[:, n:2 * n], cw[:, n:2 * n]) + cb[:, n:2 * n])
    halo_ref[:, 0:n] = xin[BLK - 8:BLK]
    halo_ref[:, n:2 * n] = bcin[BLK - 8:BLK]
    bm, cm = bcs[:, 0:gn], bcs[:, gn:2 * gn]

    ri = lax.broadcasted_iota(jnp.int32, (BLK, BLK), 0)
    ci = lax.broadcasted_iota(jnp.int32, (BLK, BLK), 1)
    incl = ri >= ci
    valid_rows = (c * BLK + ri) >= PAD
    dtp_all = jnp.where(valid_rows, _softplus(sm_ref[...] + dtb_l_ref[...]), 0.0)
    acum_all = _dot_f32(incl.astype(F32), dtp_all * (-jnp.exp(alog_l_ref[...])))
    e = e_ref[...]
    dtp_x = _dot_f32(dtp_all, e)
    acum_x = _dot_f32(acum_all, e)
    ci16 = lax.broadcasted_iota(jnp.int32, (SSD_HEADS, BLK), 1)
    dtp_rows = jnp.where((c * BLK + ci16) >= PAD, _softplus(smt_ref[16:32, :] + dtb_c_ref[...]), 0.0)
    acum_rows = _dot_f32(dtp_rows * (-jnp.exp(alog_c_ref[...])), (ri <= ci).astype(F32))

    xdt = xs * dtp_x
    eacum_x = jnp.exp(acum_x)
    alast_x = acum_x[BLK - 1:BLK, :]
    xdt_dec = xdt * jnp.exp(alast_x - acum_x)
    valid_out = (c * BLK + lax.broadcasted_iota(jnp.int32, (BLK, gw), 0)) >= PAD
    hpg = SSD_HEADS // SSD_GROUPS

    groups = range(SSD_GROUPS)
    hds = range(SSD_HEADS)
    gss = [slice(g * gw, (g + 1) * gw) for g in groups]
    bgs = [bm[:, g * SSD_N:(g + 1) * SSD_N] for g in groups]
    cgs = [cm[:, g * SSD_N:(g + 1) * SSD_N] for g in groups]
    cbms = [_dot_nt(cgs[g], bgs[g]) for g in groups]
    sts = [st_ref[:, gss[g]] for g in groups]
    yoffs = [_dot(cgs[g], sts[g]) * eacum_x[:, gss[g]] for g in groups]
    lms = [jnp.exp(jnp.where(incl, acum_all[:, 16 + hh:17 + hh] - acum_rows[hh:hh + 1, :], -jnp.inf)) for hh in hds]
    ydiag = [_dot(cbms[hh // hpg] * lms[hh], xdt[:, hh * SSD_P:(hh + 1) * SSD_P]) for hh in hds]
    for g in groups:
        st_ref[:, gss[g]] = sts[g] * jnp.exp(alast_x[:, gss[g]]) + _dot_tn(bgs[g], xdt_dec[:, gss[g]])
    for g in groups:
        gs = gss[g]
        y = yoffs[g] + jnp.concatenate(ydiag[g * hpg:(g + 1) * hpg], axis=1)
        y = y + d_ref[:, gs] * xs[:, gs]
        y = y * _silu(z_ref[:, gs])
        y = _rmsnorm(y, nw_ref[:, gs])
        o_ref[:, gs] = jnp.where(valid_out, y, 0.0).astype(o_ref.dtype)


def _ssd(u, small, small_t, bsz, nb, conv_w, conv_b, dt_bias, a_log, d_skip, norm_w):
    t = u.shape[0]
    n = SSD_HEADS * SSD_P

    def lane_vec(v, off):
        return jnp.zeros((1, SMALL_W), F32).at[0, off:off + v.shape[0]].set(v.astype(F32))

    def ublk(col):
        return pl.BlockSpec((BLK, n), lambda b, c: (b * nb + c, col // n))

    def full(shape):
        return pl.BlockSpec(shape, lambda b, c: (0,) * len(shape))

    expand = jnp.zeros((SMALL_W, n), F32).at[16:16 + SSD_HEADS].set(
        jnp.repeat(jnp.eye(SSD_HEADS, dtype=F32), SSD_P, axis=1))
    return pl.pallas_call(
        _ssd_kernel,
        grid=(bsz, nb),
        in_specs=[
            ublk(COL_SSD_Z), ublk(COL_SSD_X), ublk(COL_SSD_BC),
            pl.BlockSpec((BLK, SMALL_W), lambda b, c: (b * nb + c, 0)),
            pl.BlockSpec((SMALL_ROWS, BLK), lambda b, c: (0, b * nb + c)),
            full((4, 2 * n)), full((1, 2 * n)), full((1, SMALL_W)), full((1, SMALL_W)), full((SSD_HEADS, 1)),
            full((SSD_HEADS, 1)), full((1, n)), full((1, n)), full((SMALL_W, n)),
        ],
        out_specs=pl.BlockSpec((BLK, n), lambda b, c: (b * nb + c, 0)),
        out_shape=jax.ShapeDtypeStruct((t, n), BF16),
        scratch_shapes=[pltpu.VMEM((8, 2 * n), F32), pltpu.VMEM((SSD_N, n), F32)],
        compiler_params=_cparams(("parallel", "arbitrary")),
        name="ssd_mixer",
    )(u, u, u, small, small_t, conv_w.astype(F32), conv_b.astype(F32).reshape(1, 2 * n),
      lane_vec(a_log, 16), lane_vec(dt_bias, 16), a_log.astype(F32).reshape(SSD_HEADS, 1),
      dt_bias.astype(F32).reshape(SSD_HEADS, 1), jnp.repeat(d_skip.astype(F32), SSD_P).reshape(1, n),
      norm_w.astype(F32).reshape(1, n), expand)


def _swa_kernel(sink_ref, q_ref, kc_ref, kp_ref, km_ref, vc_ref, vp_ref, vm_ref, o_ref):
    c = pl.program_id(1)
    ri = lax.broadcasted_iota(jnp.int32, (BLK, BLK), 0)
    ci = lax.broadcasted_iota(jnp.int32, (BLK, BLK), 1)
    cur_ok = (ci <= ri) & ((c > 0) | (ci >= PAD))
    prev_ok = ((c >= 2) & (ci > ri)) | ((c == 1) & (ci >= PAD))
    meta_ok = jnp.broadcast_to(c >= 2, (BLK, N_META))
    rep = SWA_Q_HEADS // SWA_KV_HEADS
    neg = -jnp.inf
    heads = range(SWA_Q_HEADS)
    kvs = [slice((h // rep) * SWA_DH, (h // rep + 1) * SWA_DH) for h in heads]
    qhs = [q_ref[:, h * SWA_DH:(h + 1) * SWA_DH] * (SWA_DH ** -0.5) for h in heads]
    s_cs = [jnp.where(cur_ok, _dot_nt(qhs[h], kc_ref[:, kvs[h]]), neg) for h in heads]
    s_ps = [jnp.where(prev_ok, _dot_nt(qhs[h], kp_ref[:, kvs[h]]), neg) for h in heads]
    s_ms = [jnp.where(meta_ok, _dot_nt(qhs[h], km_ref[:, kvs[h]]), neg) for h in heads]
    sinks = [sink_ref[h] for h in heads]
    ms = [jnp.maximum(jnp.maximum(jnp.max(s_cs[h], axis=-1, keepdims=True), jnp.max(s_ps[h], axis=-1, keepdims=True)),
                      jnp.maximum(jnp.max(s_ms[h], axis=-1, keepdims=True), sinks[h])) for h in heads]
    e_cs = [jnp.exp(s_cs[h] - ms[h]) for h in heads]
    e_ps = [jnp.exp(s_ps[h] - ms[h]) for h in heads]
    e_ms = [jnp.exp(s_ms[h] - ms[h]) for h in heads]
    invs = [1.0 / (jnp.sum(e_cs[h], axis=-1, keepdims=True) + jnp.sum(e_ps[h], axis=-1, keepdims=True)
                   + jnp.sum(e_ms[h], axis=-1, keepdims=True) + jnp.exp(sinks[h] - ms[h])) for h in heads]
    outs = [_dot(e_cs[h] * invs[h], vc_ref[:, kvs[h]]) + _dot(e_ps[h] * invs[h], vp_ref[:, kvs[h]])
            + _dot(e_ms[h] * invs[h], vm_ref[:, kvs[h]]) for h in heads]
    for h in heads:
        o_ref[:, h * SWA_DH:(h + 1) * SWA_DH] = outs[h].astype(o_ref.dtype)


def _swa(u, bsz, nb, sinks):
    t = u.shape[0]
    nq = SWA_Q_HEADS * SWA_DH
    nkv = SWA_KV_HEADS * SWA_DH
    meta_blk = PAD // N_META
    rows16 = (nb * BLK) // N_META

    def cur(col):
        return pl.BlockSpec((BLK, nkv), lambda b, c: (b * nb + c, col // nkv))

    def prev(col):
        return pl.BlockSpec((BLK, nkv), lambda b, c: (b * nb + jnp.maximum(c - 1, 0), col // nkv))

    def meta(col):
        return pl.BlockSpec((N_META, nkv), lambda b, c: (b * rows16 + meta_blk, col // nkv))

    return pl.pallas_call(
        _swa_kernel,
        grid=(bsz, nb),
        in_specs=[
            pl.BlockSpec(memory_space=pltpu.SMEM),
            pl.BlockSpec((BLK, nq), lambda b, c: (b * nb + c, COL_SWA_Q // nq)),
            cur(COL_SWA_K), prev(COL_SWA_K), meta(COL_SWA_K),
            cur(COL_SWA_V), prev(COL_SWA_V), meta(COL_SWA_V),
        ],
        out_specs=pl.BlockSpec((BLK, nq), lambda b, c: (b * nb + c, 0)),
        out_shape=jax.ShapeDtypeStruct((t, nq), BF16),
        compiler_params=_cparams(("parallel", "parallel")),
        name="swa_mixer",
    )(sinks.astype(F32), u, u, u, u, u, u, u)


def _merge_kernel(h_ref, yg_ref, ys_ref, yc_ref, ga_ref, gb_ref, gc_ref, wg_ref, ws_ref, wc_ref, wo_ref, o_ref):
    d = functools.partial(jnp.dot, preferred_element_type=F32)
    merged = (_sigmoid(ga_ref[...]) * d(yg_ref[...], wg_ref[...])
              + _sigmoid(gb_ref[...]) * d(ys_ref[...], ws_ref[...])
              + _sigmoid(gc_ref[...]) * d(yc_ref[...], wc_ref[...]))
    o_ref[...] = h_ref[...] + d(merged.astype(BF16), wo_ref[...])


def _merge(h, y_gdn, y_ssd, y_swa, u, w_gdn, w_ssd, w_swa, w_out):
    t = h.shape[0]
    tm = _pick_tile(t, (512, 384, 256, 128))
    row = pl.BlockSpec((tm, D_MODEL), lambda i: (i, 0))
    wfull = pl.BlockSpec((D_MODEL, D_MODEL), lambda i: (0, 0))

    def gate(k):
        return pl.BlockSpec((tm, D_MODEL), lambda i: (i, COL_GATES // D_MODEL + k))

    return pl.pallas_call(
        _merge_kernel,
        grid=(t // tm,),
        in_specs=[row, row, row, row, gate(0), gate(1), gate(2), wfull, wfull, wfull, wfull],
        out_specs=row,
        out_shape=jax.ShapeDtypeStruct((t, D_MODEL), F32),
        compiler_params=_cparams(("parallel",)),
        name="merge",
    )(h, y_gdn, y_ssd, y_swa, u, u, u, w_gdn, w_ssd, w_swa, w_out)


def _mlp_kernel(h_ref, nw_ref, wu_ref, wd_ref, fw_ref, o_ref, hn_ref, acc_ref, *, final_norm):
    j = pl.program_id(1)

    @pl.when(j == 0)
    def _():
        hn_ref[...] = _rmsnorm(h_ref[...], nw_ref[...]).astype(BF16)
        acc_ref[...] = jnp.zeros_like(acc_ref)

    up = jnp.maximum(jnp.dot(hn_ref[...], wu_ref[...], preferred_element_type=F32), 0.0)
    acc_ref[...] += jnp.dot((up * up).astype(BF16), wd_ref[...], preferred_element_type=F32)

    @pl.when(j == pl.num_programs(1) - 1)
    def _():
        r = h_ref[...] + acc_ref[...]
        if final_norm:
            r = _rmsnorm(r, fw_ref[...])
        o_ref[...] = r


def _mlp(h, norm_w, w_up, w_down, final_w, final_norm):
    t = h.shape[0]
    tm = _pick_tile(t, (1024, 768, 512, 384, 256, 128))
    tf = 1024
    return pl.pallas_call(
        functools.partial(_mlp_kernel, final_norm=final_norm),
        grid=(t // tm, D_FF // tf),
        in_specs=[
            pl.BlockSpec((tm, D_MODEL), lambda i, j: (i, 0)),
            pl.BlockSpec((1, D_MODEL), lambda i, j: (0, 0)),
            pl.BlockSpec((D_MODEL, tf), lambda i, j: (0, j)),
            pl.BlockSpec((tf, D_MODEL), lambda i, j: (j, 0)),
            pl.BlockSpec((1, D_MODEL), lambda i, j: (0, 0)),
        ],
        out_specs=pl.BlockSpec((tm, D_MODEL), lambda i, j: (i, 0)),
        out_shape=jax.ShapeDtypeStruct((t, D_MODEL), F32),
        scratch_shapes=[pltpu.VMEM((tm, D_MODEL), BF16), pltpu.VMEM((tm, D_MODEL), F32)],
        compiler_params=_cparams(("parallel", "arbitrary")),
        name="mlp",
    )(h, norm_w, w_up, w_down, final_w)


def _permute_w_in(w):
    a_small = 4 * GDN_HEADS * GDN_D
    b_z = a_small + 2 * GDN_HEADS
    b_dt = b_z + 3 * SSD_HEADS * SSD_P
    c_q = b_dt + SSD_HEADS
    c_k = c_q + SWA_Q_HEADS * SWA_DH
    c_v = c_k + SWA_KV_HEADS * SWA_DH
    gates = c_v + SWA_KV_HEADS * SWA_DH
    main = jnp.concatenate([w[:, :a_small], w[:, b_z:b_dt], w[:, c_q:c_k], w[:, gates:], w[:, c_k:c_v],
                            w[:, c_v:gates]], axis=1)
    small = jnp.concatenate([w[:, a_small:b_z], w[:, b_dt:c_q],
                             jnp.zeros((w.shape[0], SMALL_W - SMALL_ROWS), w.dtype)], axis=1)
    return main.astype(BF16), small.astype(BF16)


def kernel(x, meta_tokens, norm1_w, w_in, gdn_conv_w, gdn_a_log, gdn_dt_bias, gdn_norm_w, ssd_conv_w, ssd_conv_b,
           ssd_dt_bias, ssd_a_log, ssd_d, ssd_norm_w, swa_sinks, w_proj_gdn, w_proj_ssd, w_proj_swa, w_out, norm2_w,
           w_up, w_down, final_norm_w):
    bsz, seq, dm = x.shape
    depth = w_in.shape[0]
    assert dm == D_MODEL and seq % BLK == 0
    nb = seq // BLK + 1
    lp = nb * BLK
    meta = jnp.broadcast_to(meta_tokens.astype(x.dtype)[None], (bsz, N_META, dm))
    h = jnp.concatenate([jnp.zeros((bsz, PAD, dm), x.dtype), meta, x], axis=1).reshape(bsz * lp, dm)
    final_w = final_norm_w.astype(F32).reshape(1, dm)
    for l in range(depth):
        w_main, w_small = _permute_w_in(w_in[l])
        u, small, small_t = _inproj(h, norm1_w[l].astype(F32).reshape(1, dm), w_main, w_small)
        y_gdn = _gdn(u, small, small_t, bsz, nb, gdn_conv_w[l], gdn_a_log[l], gdn_dt_bias[l], gdn_norm_w[l])
        y_ssd = _ssd(u, small, small_t, bsz, nb, ssd_conv_w[l], ssd_conv_b[l], ssd_dt_bias[l], ssd_a_log[l],
                     ssd_d[l], ssd_norm_w[l])
        y_swa = _swa(u, bsz, nb, swa_sinks[l])
        h = _merge(h, y_gdn, y_ssd, y_swa, u, w_proj_gdn[l].astype(BF16), w_proj_ssd[l].astype(BF16),
                   w_proj_swa[l].astype(BF16), w_out[l].astype(BF16))
        h = _mlp(h, norm2_w[l].astype(F32).reshape(1, dm), w_up[l].astype(BF16), w_down[l].astype(BF16), final_w,
                 final_norm=(l == depth - 1))
    return h.reshape(bsz, lp, dm)[:, BLK:]
```

```python
import functools

import jax
import jax.numpy as jnp
from jax import lax
from jax.experimental import pallas as pl
from jax.experimental.pallas import tpu as pltpu

F32 = jnp.float32
BF16 = jnp.bfloat16

D_MODEL = 1024
D_FF = 4 * D_MODEL
N_META = 16
RMS_EPS = 1e-6
L2_EPS = 1e-6

BLK = 128
PAD = BLK - N_META

GDN_HEADS = 8
GDN_D = 128
SSD_HEADS = 16
SSD_P = 64
SSD_GROUPS = 4
SSD_N = 128
SWA_Q_HEADS = 16
SWA_KV_HEADS = 4
SWA_DH = 64

COL_GDN_Q, COL_GDN_K, COL_GDN_V, COL_GDN_GATE = 0, 1024, 2048, 3072
COL_SSD_Z, COL_SSD_X, COL_SSD_BC = 4096, 5120, 6144
COL_SWA_Q = 7168
COL_GATES = 8192
COL_SWA_K, COL_SWA_V = 11264, 11520
U_WIDTH = 11776
SMALL_W = 128
SMALL_ROWS = 32
U_DTYPE = BF16

VMEM_LIMIT = 52 * 1024 * 1024


def _cparams(sem):
    return pltpu.CompilerParams(dimension_semantics=sem, vmem_limit_bytes=VMEM_LIMIT)


def _pick_tile(n, candidates):
    for c in candidates:
        if n % c == 0:
            return c
    raise ValueError(f"no tile for {n}")


def _dot(a, b):
    return jnp.dot(a.astype(BF16), b.astype(BF16), preferred_element_type=F32)


def _dot_nt(a, b):
    return lax.dot_general(a.astype(BF16), b.astype(BF16), (((1,), (1,)), ((), ())), preferred_element_type=F32)


def _dot_tn(a, b):
    return lax.dot_general(a.astype(BF16), b.astype(BF16), (((0,), (0,)), ((), ())), preferred_element_type=F32)


def _split3_bf16(a):
    p1 = a.astype(BF16)
    r1 = a - p1.astype(F32)
    p2 = r1.astype(BF16)
    p3 = (r1 - p2.astype(F32)).astype(BF16)
    return p1, p2, p3


def _dot_sel_rhs(a, sel):
    s = sel.astype(F32).astype(BF16)
    return sum(jnp.dot(p, s, preferred_element_type=F32) for p in _split3_bf16(a))


def _dot_sel_lhs(sel, b):
    s = sel.astype(F32).astype(BF16)
    return sum(jnp.dot(s, p, preferred_element_type=F32) for p in _split3_bf16(b))


def _sigmoid(x):
    return 1.0 / (1.0 + jnp.exp(-x))


def _silu(x):
    return x * _sigmoid(x)


def _softplus(x):
    return jnp.maximum(x, 0.0) + jnp.log1p(jnp.exp(-jnp.abs(x)))


def _rmsnorm(x, w):
    return x * lax.rsqrt(jnp.mean(x * x, axis=-1, keepdims=True) + RMS_EPS) * w


def _causal_conv4(x, halo, w):
    xe = jnp.concatenate([halo, x], axis=0)
    acc = x * w[3:4]
    for j in range(3):
        acc = acc + pltpu.roll(xe, 3 - j, axis=0)[8:] * w[j:j + 1]
    return acc


SOLVE_BASE = 16


def _solve_unit_lower(a_list, x_list):
    c = a_list[0].shape[0]
    ri = lax.broadcasted_iota(jnp.int32, (c, c), 0)
    ci = lax.broadcasted_iota(jnp.int32, (c, c), 1)

    def blk(x, size):
        return lax.shift_right_logical(x, size.bit_length() - 1)

    diag = blk(ri, SOLVE_BASE) == blk(ci, SOLVE_BASE)
    ps = [jnp.where(diag, a, 0.0) for a in a_list]
    ts = [jnp.where(ri == ci, 1.0, 0.0) - p for p in ps]
    size = 2
    while size < SOLVE_BASE:
        ps = [_dot(p, p) for p in ps]
        ts = [t + _dot(p, t) for p, t in zip(ps, ts)]
        size *= 2
    size = SOLVE_BASE
    while size < c:
        off = (blk(ri, 2 * size) == blk(ci, 2 * size)) & (blk(ri, size) != blk(ci, size))
        ots = [_dot(jnp.where(off, a, 0.0), t) for a, t in zip(a_list, ts)]
        ts = [t - _dot(t, ot) for t, ot in zip(ts, ots)]
        size *= 2
    return [_dot(t, x) for t, x in zip(ts, x_list)]


def _inproj_kernel(h_ref, nw_ref, w_ref, ws_ref, u_ref, small_ref, smallt_ref, hn_ref):
    @pl.when(pl.program_id(1) == 0)
    def _():
        hn = _rmsnorm(h_ref[...], nw_ref[...]).astype(BF16)
        hn_ref[...] = hn
        small = jnp.dot(hn, ws_ref[...], preferred_element_type=F32)
        small_ref[...] = small
        smallt_ref[...] = small.T[:SMALL_ROWS]

    u_ref[...] = jnp.dot(hn_ref[...], w_ref[...], preferred_element_type=F32).astype(u_ref.dtype)


def _inproj(h, norm_w, w_main, w_small):
    t = h.shape[0]
    tm = _pick_tile(t, (3072, 1536, 1024, 768, 512, 384, 256, 128))
    tn = 512
    return pl.pallas_call(
        _inproj_kernel,
        grid=(t // tm, U_WIDTH // tn),
        in_specs=[
            pl.BlockSpec((tm, D_MODEL), lambda i, j: (i, 0)),
            pl.BlockSpec((1, D_MODEL), lambda i, j: (0, 0)),
            pl.BlockSpec((D_MODEL, tn), lambda i, j: (0, j)),
            pl.BlockSpec((D_MODEL, SMALL_W), lambda i, j: (0, 0)),
        ],
        out_specs=[
            pl.BlockSpec((tm, tn), lambda i, j: (i, j)),
            pl.BlockSpec((tm, SMALL_W), lambda i, j: (i, 0)),
            pl.BlockSpec((SMALL_ROWS, tm), lambda i, j: (0, i)),
        ],
        out_shape=[
            jax.ShapeDtypeStruct((t, U_WIDTH), U_DTYPE),
            jax.ShapeDtypeStruct((t, SMALL_W), F32),
            jax.ShapeDtypeStruct((SMALL_ROWS, t), F32),
        ],
        scratch_shapes=[pltpu.VMEM((tm, D_MODEL), BF16)],
        compiler_params=_cparams(("parallel", "arbitrary")),
        name="inproj",
    )(h, norm_w, w_main, w_small)


def _gdn_kernel(q_ref, k_ref, v_ref, gate_ref, sm_ref, smt_ref, cw_ref, alog_l_ref, dtb_l_ref, alog_c_ref,
                dtb_c_ref, nw_ref, o_ref, halo_ref, s_ref):
    c = pl.program_id(1)

    @pl.when(c == 0)
    def _():
        halo_ref[...] = jnp.zeros_like(halo_ref)
        s_ref[...] = jnp.zeros_like(s_ref)

    w_all = 3 * GDN_HEADS * GDN_D
    cw = cw_ref[...]
    halo = halo_ref[...]
    qin, kin, vin = q_ref[...].astype(F32), k_ref[...].astype(F32), v_ref[...].astype(F32)
    n = GDN_HEADS * GDN_D
    qc = _silu(_causal_conv4(qin, halo[:, 0:n], cw[:, 0:n]))
    kc = _silu(_causal_conv4(kin, halo[:, n:2 * n], cw[:, n:2 * n]))
    vc = _silu(_causal_conv4(vin, halo[:, 2 * n:w_all], cw[:, 2 * n:w_all]))
    halo_ref[:, 0:n] = qin[BLK - 8:BLK]
    halo_ref[:, n:2 * n] = kin[BLK - 8:BLK]
    halo_ref[:, 2 * n:w_all] = vin[BLK - 8:BLK]

    ri = lax.broadcasted_iota(jnp.int32, (BLK, BLK), 0)
    ci = lax.broadcasted_iota(jnp.int32, (BLK, BLK), 1)
    incl = ri >= ci
    strict = ri > ci
    valid_rows = (c * BLK + ri) >= PAD
    sm = sm_ref[...]
    beta_all = jnp.where(valid_rows, _sigmoid(sm), 0.0)
    g_all = jnp.where(valid_rows, -jnp.exp(alog_l_ref[...]) * _softplus(sm + dtb_l_ref[...]), 0.0)
    gam_all = _dot_sel_lhs(incl, g_all)
    ci8 = lax.broadcasted_iota(jnp.int32, (GDN_HEADS, BLK), 1)
    g_rows = jnp.where((c * BLK + ci8) >= PAD,
                       -jnp.exp(alog_c_ref[...]) * _softplus(smt_ref[8:16, :] + dtb_c_ref[...]), 0.0)
    gam_rows = _dot_sel_rhs(g_rows, ri <= ci)
    nw = nw_ref[...]

    heads = range(GDN_HEADS)
    sls = [slice(h * GDN_D, (h + 1) * GDN_D) for h in heads]
    qs = [qc[:, sl] for sl in sls]
    ks = [kc[:, sl] for sl in sls]
    qs = [q * lax.rsqrt(jnp.sum(q * q, axis=-1, keepdims=True) + L2_EPS) * (GDN_D ** -0.5) for q in qs]
    ks = [k * lax.rsqrt(jnp.sum(k * k, axis=-1, keepdims=True) + L2_EPS) for k in ks]
    betas = [beta_all[:, h:h + 1] for h in heads]
    gam_cs = [gam_all[:, 8 + h:9 + h] for h in heads]
    dmats = [jnp.exp(jnp.where(incl, gam_cs[h] - gam_rows[h:h + 1, :], -jnp.inf)) for h in heads]
    kbs = [ks[h] * betas[h] for h in heads]
    egams = [jnp.exp(g) for g in gam_cs]
    a_list = [jnp.where(strict, _dot_nt(kbs[h], ks[h]) * dmats[h], 0.0) for h in heads]
    rhs = [jnp.concatenate([vc[:, sls[h]] * betas[h], kbs[h] * egams[h]], axis=1) for h in heads]
    sols = _solve_unit_lower(a_list, rhs)
    aqks = [_dot_nt(qs[h], ks[h]) * dmats[h] for h in heads]
    glasts = [g[BLK - 1:BLK, :] for g in gam_cs]
    ktails = [ks[h] * jnp.exp(glasts[h] - gam_cs[h]) for h in heads]
    states = [s_ref[h] for h in heads]
    wss = [_dot(jnp.concatenate([sols[h][:, GDN_D:], qs[h] * egams[h]], axis=0), states[h]) for h in heads]
    vnews = [sols[h][:, :GDN_D] - wss[h][:BLK] for h in heads]
    outs = [wss[h][BLK:] + _dot(aqks[h], vnews[h]) for h in heads]
    for h in heads:
        s_ref[h] = states[h] * jnp.exp(glasts[h]) + _dot_tn(ktails[h], vnews[h])
    for h in heads:
        gt = gate_ref[:, sls[h]].astype(F32)
        o_ref[:, sls[h]] = (_rmsnorm(outs[h], nw) * _silu(gt)).astype(o_ref.dtype)


def _gdn(u, small, small_t, bsz, nb, conv_w, a_log, dt_bias, norm_w):
    t = u.shape[0]
    n = GDN_HEADS * GDN_D

    def lane_vec(v, off):
        return jnp.zeros((1, SMALL_W), F32).at[0, off:off + v.shape[0]].set(v.astype(F32))

    def ublk(col):
        return pl.BlockSpec((BLK, n), lambda b, c: (b * nb + c, col // n))

    def full(shape):
        return pl.BlockSpec(shape, lambda b, c: (0,) * len(shape))

    return pl.pallas_call(
        _gdn_kernel,
        grid=(bsz, nb),
        in_specs=[
            ublk(COL_GDN_Q), ublk(COL_GDN_K), ublk(COL_GDN_V), ublk(COL_GDN_GATE),
            pl.BlockSpec((BLK, SMALL_W), lambda b, c: (b * nb + c, 0)),
            pl.BlockSpec((SMALL_ROWS, BLK), lambda b, c: (0, b * nb + c)),
            full((4, 3 * n)), full((1, SMALL_W)), full((1, SMALL_W)), full((GDN_HEADS, 1)), full((GDN_HEADS, 1)),
            full((1, GDN_D)),
        ],
        out_specs=pl.BlockSpec((BLK, n), lambda b, c: (b * nb + c, 0)),
        out_shape=jax.ShapeDtypeStruct((t, n), BF16),
        scratch_shapes=[pltpu.VMEM((8, 3 * n), F32), pltpu.VMEM((GDN_HEADS, GDN_D, GDN_D), F32)],
        compiler_params=_cparams(("parallel", "arbitrary")),
        name="gdn_mixer",
    )(u, u, u, u, small, small_t, conv_w.astype(F32), lane_vec(a_log, 8), lane_vec(dt_bias, 8),
      a_log.astype(F32).reshape(GDN_HEADS, 1), dt_bias.astype(F32).reshape(GDN_HEADS, 1),
      norm_w.astype(F32).reshape(1, GDN_D))


def _ssd_kernel(z_ref, x_ref, bc_ref, sm_ref, smt_ref, cw_ref, cb_ref, alog_l_ref, dtb_l_ref, alog_c_ref,
                dtb_c_ref, d_ref, nw_ref, e_ref, o_ref, halo_ref, st_ref):
    c = pl.program_id(1)

    @pl.when(c == 0)
    def _():
        halo_ref[...] = jnp.zeros_like(halo_ref)
        st_ref[...] = jnp.zeros_like(st_ref)

    n = SSD_HEADS * SSD_P
    gn = SSD_GROUPS * SSD_N
    gw = n // SSD_GROUPS
    cw = cw_ref[...]
    cb = cb_ref[...]
    halo = halo_ref[...]
    xin, bcin = x_ref[...].astype(F32), bc_ref[...].astype(F32)
    xs = _silu(_causal_conv4(xin, halo[:, 0:n], cw[:, 0:n]) + cb[:, 0:n])
    bcs = _silu(_causal_conv4(bcin, halo[:, n:2 * n], cw[:, n:2 * n]) + cb[:, n:2 * n])
    halo_ref[:, 0:n] = xin[BLK - 8:BLK]
    halo_ref[:, n:2 * n] = bcin[BLK - 8:BLK]
    bm, cm = bcs[:, 0:gn], bcs[:, gn:2 * gn]

    ri = lax.broadcasted_iota(jnp.int32, (BLK, BLK), 0)
    ci = lax.broadcasted_iota(jnp.int32, (BLK, BLK), 1)
    incl = ri >= ci
    valid_rows = (c * BLK + ri) >= PAD
    dtp_all = jnp.where(valid_rows, _softplus(sm_ref[...] + dtb_l_ref[...]), 0.0)
    acum_all = _dot_sel_lhs(incl, dtp_all * (-jnp.exp(alog_l_ref[...])))
    e = e_ref[...]
    dtp_x = _dot_sel_rhs(dtp_all, e)
    acum_x = _dot_sel_rhs(acum_all, e)
    ci16 = lax.broadcasted_iota(jnp.int32, (SSD_HEADS, BLK), 1)
    dtp_rows = jnp.where((c * BLK + ci16) >= PAD, _softplus(smt_ref[16:32, :] + dtb_c_ref[...]), 0.0)
    acum_rows = _dot_sel_rhs(dtp_rows * (-jnp.exp(alog_c_ref[...])), ri <= ci)

    xdt = xs * dtp_x
    eacum_x = jnp.exp(acum_x)
    alast_x = acum_x[BLK - 1:BLK, :]
    xdt_dec = xdt * jnp.exp(alast_x - acum_x)
    valid_out = (c * BLK + lax.broadcasted_iota(jnp.int32, (BLK, gw), 0)) >= PAD
    hpg = SSD_HEADS // SSD_GROUPS

    groups = range(SSD_GROUPS)
    hds = range(SSD_HEADS)
    gss = [slice(g * gw, (g + 1) * gw) for g in groups]
    bgs = [bm[:, g * SSD_N:(g + 1) * SSD_N] for g in groups]
    cgs = [cm[:, g * SSD_N:(g + 1) * SSD_N] for g in groups]
    cbms = [_dot_nt(cgs[g], bgs[g]) for g in groups]
    sts = [st_ref[:, gss[g]] for g in groups]
    yoffs = [_dot(cgs[g], sts[g]) * eacum_x[:, gss[g]] for g in groups]
    lms = [jnp.exp(jnp.where(incl, acum_all[:, 16 + hh:17 + hh] - acum_rows[hh:hh + 1, :], -jnp.inf)) for hh in hds]
    ydiag = [_dot(cbms[hh // hpg] * lms[hh], xdt[:, hh * SSD_P:(hh + 1) * SSD_P]) for hh in hds]
    for g in groups:
        st_ref[:, gss[g]] = sts[g] * jnp.exp(alast_x[:, gss[g]]) + _dot_tn(bgs[g], xdt_dec[:, gss[g]])
    for g in groups:
        gs = gss[g]
        y = yoffs[g] + jnp.concatenate(ydiag[g * hpg:(g + 1) * hpg], axis=1)
        y = y + d_ref[:, gs] * xs[:, gs]
        y = y * _silu(z_ref[:, gs].astype(F32))
        y = _rmsnorm(y, nw_ref[:, gs])
        o_ref[:, gs] = jnp.where(valid_out, y, 0.0).astype(o_ref.dtype)


def _ssd(u, small, small_t, bsz, nb, conv_w, conv_b, dt_bias, a_log, d_skip, norm_w):
    t = u.shape[0]
    n = SSD_HEADS * SSD_P

    def lane_vec(v, off):
        return jnp.zeros((1, SMALL_W), F32).at[0, off:off + v.shape[0]].set(v.astype(F32))

    def ublk(col):
        return pl.BlockSpec((BLK, n), lambda b, c: (b * nb + c, col // n))

    def full(shape):
        return pl.BlockSpec(shape, lambda b, c: (0,) * len(shape))

    expand = jnp.zeros((SMALL_W, n), F32).at[16:16 + SSD_HEADS].set(
        jnp.repeat(jnp.eye(SSD_HEADS, dtype=F32), SSD_P, axis=1))
    return pl.pallas_call(
        _ssd_kernel,
        grid=(bsz, nb),
        in_specs=[
            ublk(COL_SSD_Z), ublk(COL_SSD_X), ublk(COL_SSD_BC),
            pl.BlockSpec((BLK, SMALL_W), lambda b, c: (b * nb + c, 0)),
            pl.BlockSpec((SMALL_ROWS, BLK), lambda b, c: (0, b * nb + c)),
            full((4, 2 * n)), full((1, 2 * n)), full((1, SMALL_W)), full((1, SMALL_W)), full((SSD_HEADS, 1)),
            full((SSD_HEADS, 1)), full((1, n)), full((1, n)), full((SMALL_W, n)),
        ],
        out_specs=pl.BlockSpec((BLK, n), lambda b, c: (b * nb + c, 0)),
        out_shape=jax.ShapeDtypeStruct((t, n), BF16),
        scratch_shapes=[pltpu.VMEM((8, 2 * n), F32), pltpu.VMEM((SSD_N, n), F32)],
        compiler_params=_cparams(("parallel", "arbitrary")),
        name="ssd_mixer",
    )(u, u, u, small, small_t, conv_w.astype(F32), conv_b.astype(F32).reshape(1, 2 * n),
      lane_vec(a_log, 16), lane_vec(dt_bias, 16), a_log.astype(F32).reshape(SSD_HEADS, 1),
      dt_bias.astype(F32).reshape(SSD_HEADS, 1), jnp.repeat(d_skip.astype(F32), SSD_P).reshape(1, n),
      norm_w.astype(F32).reshape(1, n), expand)


def _swa_kernel(sink_ref, q_ref, kc_ref, kp_ref, km_ref, vc_ref, vp_ref, vm_ref, o_ref):
    c = pl.program_id(1)
    ri = lax.broadcasted_iota(jnp.int32, (BLK, BLK), 0)
    ci = lax.broadcasted_iota(jnp.int32, (BLK, BLK), 1)
    cur_ok = (ci <= ri) & ((c > 0) | (ci >= PAD))
    prev_ok = ((c >= 2) & (ci > ri)) | ((c == 1) & (ci >= PAD))
    rep = SWA_Q_HEADS // SWA_KV_HEADS
    neg = -jnp.inf
    bias_pc = jnp.concatenate([jnp.where(prev_ok, 0.0, neg), jnp.where(cur_ok, 0.0, neg)], axis=1)
    bias_pc = jnp.concatenate([bias_pc] * rep, axis=0)
    bias_m = jnp.where(c >= 2, 0.0, neg)
    lo = ci < SWA_DH
    lo_m = lax.broadcasted_iota(jnp.int32, (N_META, BLK), 1) < SWA_DH
    row4 = lax.broadcasted_iota(jnp.int32, (rep * BLK, 1), 0)

    def both_halves(ref, hk, lo_mask):
        x = ref[:, (hk // 2) * BLK:(hk // 2 + 1) * BLK].astype(F32)
        x = jnp.where(lo_mask, x, 0.0) if hk % 2 == 0 else jnp.where(lo_mask, 0.0, x)
        return x + pltpu.roll(x, SWA_DH, axis=1)

    def stacked_queries(hk):
        scale = SWA_DH ** -0.5
        qa = q_ref[:, (2 * hk) * BLK:(2 * hk + 1) * BLK].astype(F32) * scale
        qb = q_ref[:, (2 * hk + 1) * BLK:(2 * hk + 2) * BLK].astype(F32) * scale
        return jnp.concatenate([jnp.where(lo, qa, 0.0), jnp.where(lo, 0.0, qa),
                                jnp.where(lo, qb, 0.0), jnp.where(lo, 0.0, qb)], axis=0)

    kvh = range(SWA_KV_HEADS)
    qs = [stacked_queries(hk) for hk in kvh]
    k_pc = [jnp.concatenate([both_halves(kp_ref, hk, lo), both_halves(kc_ref, hk, lo)], axis=0) for hk in kvh]
    k_m = [both_halves(km_ref, hk, lo_m) for hk in kvh]
    v_pc = [jnp.concatenate([both_halves(vp_ref, hk, lo), both_halves(vc_ref, hk, lo)], axis=0) for hk in kvh]
    v_m = [both_halves(vm_ref, hk, lo_m) for hk in kvh]
    s_pc = [_dot_nt(qs[hk], k_pc[hk]) + bias_pc for hk in kvh]
    s_m = [_dot_nt(qs[hk], k_m[hk]) + bias_m for hk in kvh]
    def sink_column(hk):
        col = jnp.full((rep * BLK, 1), sink_ref[hk * rep + rep - 1], F32)
        for r in range(rep - 2, -1, -1):
            col = jnp.where(row4 < (r + 1) * BLK, sink_ref[hk * rep + r], col)
        return col

    sinks = [sink_column(hk) for hk in kvh]
    ms = [jnp.maximum(jnp.maximum(jnp.max(s_pc[hk], axis=-1, keepdims=True), jnp.max(s_m[hk], axis=-1, keepdims=True)),
                      sinks[hk]) for hk in kvh]
    e_pc = [jnp.exp(s_pc[hk] - ms[hk]) for hk in kvh]
    e_m = [jnp.exp(s_m[hk] - ms[hk]) for hk in kvh]
    invs = [1.0 / (jnp.sum(e_pc[hk], axis=-1, keepdims=True) + jnp.sum(e_m[hk], axis=-1, keepdims=True)
                   + jnp.exp(sinks[hk] - ms[hk])) for hk in kvh]
    outs = [_dot(e_pc[hk] * invs[hk], v_pc[hk]) + _dot(e_m[hk] * invs[hk], v_m[hk]) for hk in kvh]
    for hk in kvh:
        o = outs[hk]
        o_ref[:, (2 * hk) * BLK:(2 * hk + 1) * BLK] = jnp.where(lo, o[0:BLK], o[BLK:2 * BLK]).astype(o_ref.dtype)
        o_ref[:, (2 * hk + 1) * BLK:(2 * hk + 2) * BLK] = jnp.where(lo, o[2 * BLK:3 * BLK],
                                                                     o[3 * BLK:]).astype(o_ref.dtype)


def _swa(u, bsz, nb, sinks):
    t = u.shape[0]
    nq = SWA_Q_HEADS * SWA_DH
    nkv = SWA_KV_HEADS * SWA_DH
    meta_blk = PAD // N_META
    rows16 = (nb * BLK) // N_META

    def cur(col):
        return pl.BlockSpec((BLK, nkv), lambda b, c: (b * nb + c, col // nkv))

    def prev(col):
        return pl.BlockSpec((BLK, nkv), lambda b, c: (b * nb + jnp.maximum(c - 1, 0), col // nkv))

    def meta(col):
        return pl.BlockSpec((N_META, nkv), lambda b, c: (b * rows16 + meta_blk, col // nkv))

    return pl.pallas_call(
        _swa_kernel,
        grid=(bsz, nb),
        in_specs=[
            pl.BlockSpec(memory_space=pltpu.SMEM),
            pl.BlockSpec((BLK, nq), lambda b, c: (b * nb + c, COL_SWA_Q // nq)),
            cur(COL_SWA_K), prev(COL_SWA_K), meta(COL_SWA_K),
            cur(COL_SWA_V), prev(COL_SWA_V), meta(COL_SWA_V),
        ],
        out_specs=pl.BlockSpec((BLK, nq), lambda b, c: (b * nb + c, 0)),
        out_shape=jax.ShapeDtypeStruct((t, nq), BF16),
        compiler_params=_cparams(("parallel", "parallel")),
        name="swa_mixer",
    )(sinks.astype(F32), u, u, u, u, u, u, u)


def _merge_kernel(h_ref, yg_ref, ys_ref, yc_ref, ga_ref, gb_ref, gc_ref, wg_ref, ws_ref, wc_ref, wo_ref, o_ref):
    d = functools.partial(jnp.dot, preferred_element_type=F32)
    merged = (_sigmoid(ga_ref[...].astype(F32)) * d(yg_ref[...], wg_ref[...])
              + _sigmoid(gb_ref[...].astype(F32)) * d(ys_ref[...], ws_ref[...])
              + _sigmoid(gc_ref[...].astype(F32)) * d(yc_ref[...], wc_ref[...]))
    o_ref[...] = h_ref[...] + d(merged.astype(BF16), wo_ref[...])


def _merge(h, y_gdn, y_ssd, y_swa, u, w_gdn, w_ssd, w_swa, w_out):
    t = h.shape[0]
    tm = _pick_tile(t, (512, 384, 256, 128))
    row = pl.BlockSpec((tm, D_MODEL), lambda i: (i, 0))
    wfull = pl.BlockSpec((D_MODEL, D_MODEL), lambda i: (0, 0))

    def gate(k):
        return pl.BlockSpec((tm, D_MODEL), lambda i: (i, COL_GATES // D_MODEL + k))

    return pl.pallas_call(
        _merge_kernel,
        grid=(t // tm,),
        in_specs=[row, row, row, row, gate(0), gate(1), gate(2), wfull, wfull, wfull, wfull],
        out_specs=row,
        out_shape=jax.ShapeDtypeStruct((t, D_MODEL), F32),
        compiler_params=_cparams(("parallel",)),
        name="merge",
    )(h, y_gdn, y_ssd, y_swa, u, u, u, w_gdn, w_ssd, w_swa, w_out)


def _mlp_kernel(h_ref, nw_ref, wu_ref, wd_ref, fw_ref, o_ref, hn_ref, acc_ref, *, final_norm):
    j = pl.program_id(1)

    @pl.when(j == 0)
    def _():
        hn_ref[...] = _rmsnorm(h_ref[...], nw_ref[...]).astype(BF16)
        acc_ref[...] = jnp.zeros_like(acc_ref)

    up = jnp.maximum(jnp.dot(hn_ref[...], wu_ref[...], preferred_element_type=F32), 0.0)
    acc_ref[...] += jnp.dot((up * up).astype(BF16), wd_ref[...], preferred_element_type=F32)

    @pl.when(j == pl.num_programs(1) - 1)
    def _():
        r = h_ref[...] + acc_ref[...]
        if final_norm:
            r = _rmsnorm(r, fw_ref[...])
        o_ref[...] = r


def _mlp(h, norm_w, w_up, w_down, final_w, final_norm):
    t = h.shape[0]
    tm = _pick_tile(t, (1024, 768, 512, 384, 256, 128))
    tf = 1024
    return pl.pallas_call(
        functools.partial(_mlp_kernel, final_norm=final_norm),
        grid=(t // tm, D_FF // tf),
        in_specs=[
            pl.BlockSpec((tm, D_MODEL), lambda i, j: (i, 0)),
            pl.BlockSpec((1, D_MODEL), lambda i, j: (0, 0)),
            pl.BlockSpec((D_MODEL, tf), lambda i, j: (0, j)),
            pl.BlockSpec((tf, D_MODEL), lambda i, j: (j, 0)),
            pl.BlockSpec((1, D_MODEL), lambda i, j: (0, 0)),
        ],
        out_specs=pl.BlockSpec((tm, D_MODEL), lambda i, j: (i, 0)),
        out_shape=jax.ShapeDtypeStruct((t, D_MODEL), F32),
        scratch_shapes=[pltpu.VMEM((tm, D_MODEL), BF16), pltpu.VMEM((tm, D_MODEL), F32)],
        compiler_params=_cparams(("parallel", "arbitrary")),
        name="mlp",
    )(h, norm_w, w_up, w_down, final_w)


def _permute_w_in(w):
    a_small = 4 * GDN_HEADS * GDN_D
    b_z = a_small + 2 * GDN_HEADS
    b_dt = b_z + 3 * SSD_HEADS * SSD_P
    c_q = b_dt + SSD_HEADS
    c_k = c_q + SWA_Q_HEADS * SWA_DH
    c_v = c_k + SWA_KV_HEADS * SWA_DH
    gates = c_v + SWA_KV_HEADS * SWA_DH
    main = jnp.concatenate([w[:, :a_small], w[:, b_z:b_dt], w[:, c_q:c_k], w[:, gates:], w[:, c_k:c_v],
                            w[:, c_v:gates]], axis=1)
    small = jnp.concatenate([w[:, a_small:b_z], w[:, b_dt:c_q],
                             jnp.zeros((w.shape[0], SMALL_W - SMALL_ROWS), w.dtype)], axis=1)
    return main.astype(BF16), small.astype(BF16)


def kernel(x, meta_tokens, norm1_w, w_in, gdn_conv_w, gdn_a_log, gdn_dt_bias, gdn_norm_w, ssd_conv_w, ssd_conv_b,
           ssd_dt_bias, ssd_a_log, ssd_d, ssd_norm_w, swa_sinks, w_proj_gdn, w_proj_ssd, w_proj_swa, w_out, norm2_w,
           w_up, w_down, final_norm_w):
    bsz, seq, dm = x.shape
    depth = w_in.shape[0]
    assert dm == D_MODEL and seq % BLK == 0
    nb = seq // BLK + 1
    lp = nb * BLK
    meta = jnp.broadcast_to(meta_tokens.astype(x.dtype)[None], (bsz, N_META, dm))
    h = jnp.concatenate([jnp.zeros((bsz, PAD, dm), x.dtype), meta, x], axis=1).reshape(bsz * lp, dm)
    final_w = final_norm_w.astype(F32).reshape(1, dm)
    for l in range(depth):
        w_main, w_small = _permute_w_in(w_in[l])
        u, small, small_t = _inproj(h, norm1_w[l].astype(F32).reshape(1, dm), w_main, w_small)
        y_gdn = _gdn(u, small, small_t, bsz, nb, gdn_conv_w[l], gdn_a_log[l], gdn_dt_bias[l], gdn_norm_w[l])
        y_ssd = _ssd(u, small, small_t, bsz, nb, ssd_conv_w[l], ssd_conv_b[l], ssd_dt_bias[l], ssd_a_log[l],
                     ssd_d[l], ssd_norm_w[l])
        y_swa = _swa(u, bsz, nb, swa_sinks[l])
        h = _merge(h, y_gdn, y_ssd, y_swa, u, w_proj_gdn[l].astype(BF16), w_proj_ssd[l].astype(BF16),
                   w_proj_swa[l].astype(BF16), w_out[l].astype(BF16))
        h = _mlp(h, norm2_w[l].astype(F32).reshape(1, dm), w_up[l].astype(BF16), w_down[l].astype(BF16), final_w,
                 final_norm=(l == depth - 1))
    return h.reshape(bsz, lp, dm)[:, BLK:]
```

```python
import functools

import jax
import jax.numpy as jnp
from jax import lax
from jax.experimental import pallas as pl
from jax.experimental.pallas import tpu as pltpu

F32 = jnp.float32
BF16 = jnp.bfloat16

D_MODEL = 1024
D_FF = 4 * D_MODEL
N_META = 16
RMS_EPS = 1e-6
L2_EPS = 1e-6

BLK = 128
PAD = BLK - N_META
HALO = 16

GDN_HEADS = 8
GDN_D = 128
GDN_GROUP = 2
SSD_HEADS = 16
SSD_P = 64
SSD_GROUPS = 4
SSD_N = 128
SSD_GROUP = 1
SWA_Q_HEADS = 16
SWA_KV_HEADS = 4
SWA_DH = 64

COL_GDN_Q, COL_GDN_K, COL_GDN_V, COL_GDN_GATE = 0, 1024, 2048, 3072
COL_SSD_Z, COL_SSD_X, COL_SSD_BC = 4096, 5120, 6144
COL_SWA_Q = 7168
COL_GATES = 8192
COL_SWA_K, COL_SWA_V = 11264, 11520
U_WIDTH = 11776
SMALL_W = 128
SMALL_ROWS = 32
U_DTYPE = BF16

VMEM_LIMIT = 52 * 1024 * 1024


def _cparams(sem):
    return pltpu.CompilerParams(dimension_semantics=sem, vmem_limit_bytes=VMEM_LIMIT)


def _pick_tile(n, candidates):
    for c in candidates:
        if n % c == 0:
            return c
    raise ValueError(f"no tile for {n}")


def _dot(a, b):
    return jnp.dot(a.astype(BF16), b.astype(BF16), preferred_element_type=F32)


def _dot_nt(a, b):
    return lax.dot_general(a.astype(BF16), b.astype(BF16), (((1,), (1,)), ((), ())), preferred_element_type=F32)


def _dot_tn(a, b):
    return lax.dot_general(a.astype(BF16), b.astype(BF16), (((0,), (0,)), ((), ())), preferred_element_type=F32)


def _split3_bf16(a):
    p1 = a.astype(BF16)
    r1 = a - p1.astype(F32)
    p2 = r1.astype(BF16)
    p3 = (r1 - p2.astype(F32)).astype(BF16)
    return p1, p2, p3


def _dot_sel_rhs(a, sel):
    s = sel.astype(F32).astype(BF16)
    return sum(jnp.dot(p, s, preferred_element_type=F32) for p in _split3_bf16(a))


def _dot_sel_lhs(sel, b):
    s = sel.astype(F32).astype(BF16)
    return sum(jnp.dot(s, p, preferred_element_type=F32) for p in _split3_bf16(b))


def _sigmoid(x):
    return 0.5 * jnp.tanh(0.5 * x) + 0.5


def _silu(x):
    h = 0.5 * x
    return h * jnp.tanh(h) + h


def _softplus(x):
    return jnp.maximum(x, 0.0) + jnp.log1p(jnp.exp(-jnp.abs(x)))


def _rmsnorm(x, w):
    return x * lax.rsqrt(jnp.mean(x * x, axis=-1, keepdims=True) + RMS_EPS) * w


def _shift_select(dtype):
    r = jnp.arange(3 * BLK)[:, None]
    m = jnp.arange(HALO + BLK)[None, :]
    return (m == HALO + r % BLK - 3 + r // BLK).astype(dtype)


def _causal_conv4(x, halo, w, sel):
    c = x.shape[0]
    z = jnp.dot(sel, jnp.concatenate([halo, x], axis=0), preferred_element_type=F32)
    return x.astype(F32) * w[3:4] + z[0:c] * w[0:1] + z[c:2 * c] * w[1:2] + z[2 * c:] * w[2:3]


SOLVE_BASE = 16


def _solve_unit_lower(a_list, x_list):
    c = a_list[0].shape[0]
    ri = lax.broadcasted_iota(jnp.int32, (c, c), 0)
    ci = lax.broadcasted_iota(jnp.int32, (c, c), 1)

    def blk(x, size):
        return lax.shift_right_logical(x, size.bit_length() - 1)

    diag = blk(ri, SOLVE_BASE) == blk(ci, SOLVE_BASE)
    ps = [jnp.where(diag, a, 0.0) for a in a_list]
    ts = [jnp.where(ri == ci, 1.0, 0.0) - p for p in ps]
    size = 2
    while size < SOLVE_BASE:
        ps = [_dot(p, p) for p in ps]
        ts = [t + _dot(p, t) for p, t in zip(ps, ts)]
        size *= 2
    size = SOLVE_BASE
    while size < c:
        off = (blk(ri, 2 * size) == blk(ci, 2 * size)) & (blk(ri, size) != blk(ci, size))
        ots = [_dot(jnp.where(off, a, 0.0), t) for a, t in zip(a_list, ts)]
        ts = [t - _dot(t, ot) for t, ot in zip(ts, ots)]
        size *= 2
    return [_dot(t, x) for t, x in zip(ts, x_list)]


def _inproj_kernel(h_ref, nw_ref, w_ref, ws_ref, u_ref, small_ref, smallt_ref, hn_ref):
    @pl.when(pl.program_id(1) == 0)
    def _():
        hn = _rmsnorm(h_ref[...], nw_ref[...]).astype(BF16)
        hn_ref[...] = hn
        small = jnp.dot(hn, ws_ref[...], preferred_element_type=F32)
        small_ref[...] = small
        smallt_ref[...] = small.T[:SMALL_ROWS]

    u_ref[...] = jnp.dot(hn_ref[...], w_ref[...], preferred_element_type=F32).astype(u_ref.dtype)


def _inproj(h, norm_w, w_main, w_small):
    t = h.shape[0]
    tm = _pick_tile(t, (3072, 1536, 1024, 768, 512, 384, 256, 128))
    tn = 512
    return pl.pallas_call(
        _inproj_kernel,
        grid=(t // tm, U_WIDTH // tn),
        in_specs=[
            pl.BlockSpec((tm, D_MODEL), lambda i, j: (i, 0)),
            pl.BlockSpec((1, D_MODEL), lambda i, j: (0, 0)),
            pl.BlockSpec((D_MODEL, tn), lambda i, j: (0, j)),
            pl.BlockSpec((D_MODEL, SMALL_W), lambda i, j: (0, 0)),
        ],
        out_specs=[
            pl.BlockSpec((tm, tn), lambda i, j: (i, j)),
            pl.BlockSpec((tm, SMALL_W), lambda i, j: (i, 0)),
            pl.BlockSpec((SMALL_ROWS, tm), lambda i, j: (0, i)),
        ],
        out_shape=[
            jax.ShapeDtypeStruct((t, U_WIDTH), U_DTYPE),
            jax.ShapeDtypeStruct((t, SMALL_W), F32),
            jax.ShapeDtypeStruct((SMALL_ROWS, t), F32),
        ],
        scratch_shapes=[pltpu.VMEM((tm, D_MODEL), BF16)],
        compiler_params=_cparams(("parallel", "arbitrary")),
        name="inproj",
    )(h, norm_w, w_main, w_small)


def _gdn_kernel(q_ref, k_ref, v_ref, gate_ref, sm_ref, *rest):
    smt_refs = rest[:GDN_GROUP]
    sel_ref, cw_ref, alog_l_ref, dtb_l_ref, alog_c_ref, dtb_c_ref, nw_ref, o_ref, halo_ref, s_ref = rest[GDN_GROUP:]
    c = pl.program_id(1)

    @pl.when(c == 0)
    def _():
        halo_ref[...] = jnp.zeros_like(halo_ref)
        s_ref[...] = jnp.zeros_like(s_ref)

    n = GDN_HEADS * GDN_D
    cw = cw_ref[...]
    sel = sel_ref[...]
    ri = lax.broadcasted_iota(jnp.int32, (BLK, BLK), 0)
    ci = lax.broadcasted_iota(jnp.int32, (BLK, BLK), 1)
    incl = ri >= ci
    strict = ri > ci
    valid_rows = (c * BLK + ri) >= PAD
    ci8 = lax.broadcasted_iota(jnp.int32, (GDN_HEADS, BLK), 1)
    nw = nw_ref[...]
    seqs = range(GDN_GROUP)

    raw = [(q_ref[g], k_ref[g], v_ref[g]) for g in seqs]
    halos = [halo_ref[g] for g in seqs]
    conv = [[_silu(_causal_conv4(raw[g][j], halos[g][:, j * n:(j + 1) * n], cw[:, j * n:(j + 1) * n], sel))
             for j in range(3)] for g in seqs]
    for g in seqs:
        for j in range(3):
            halo_ref[g, :, j * n:(j + 1) * n] = raw[g][j][BLK - HALO:BLK]

    sms = [sm_ref[g] for g in seqs]
    beta_all = [jnp.where(valid_rows, _sigmoid(sm), 0.0) for sm in sms]
    g_all = [jnp.where(valid_rows, -jnp.exp(alog_l_ref[...]) * _softplus(sm + dtb_l_ref[...]), 0.0) for sm in sms]
    gam_all = [_dot_sel_lhs(incl, x) for x in g_all]
    g_rows = [jnp.where((c * BLK + ci8) >= PAD,
                        -jnp.exp(alog_c_ref[...]) * _softplus(smt_refs[g][8:16, :] + dtb_c_ref[...]), 0.0)
              for g in seqs]
    gam_rows = [_dot_sel_rhs(x, ri <= ci) for x in g_rows]

    items = [(g, h) for g in seqs for h in range(GDN_HEADS)]
    idx = range(len(items))
    sls = [slice(h * GDN_D, (h + 1) * GDN_D) for _, h in items]
    qs = [conv[g][0][:, sls[i]] for i, (g, h) in enumerate(items)]
    ks = [conv[g][1][:, sls[i]] for i, (g, h) in enumerate(items)]
    vs = [conv[g][2][:, sls[i]] for i, (g, h) in enumerate(items)]
    qs = [q * lax.rsqrt(jnp.sum(q * q, axis=-1, keepdims=True) + L2_EPS) * (GDN_D ** -0.5) for q in qs]
    ks = [k * lax.rsqrt(jnp.sum(k * k, axis=-1, keepdims=True) + L2_EPS) for k in ks]
    betas = [beta_all[g][:, h:h + 1] for g, h in items]
    gam_cs = [gam_all[g][:, 8 + h:9 + h] for g, h in items]
    dmats = [jnp.exp(jnp.where(incl, gam_cs[i] - gam_rows[g][h:h + 1, :], -jnp.inf)) for i, (g, h) in enumerate(items)]
    kbs = [ks[i] * betas[i] for i in idx]
    egams = [jnp.exp(x) for x in gam_cs]
    a_list = [jnp.where(strict, _dot_nt(kbs[i], ks[i]) * dmats[i], 0.0) for i in idx]
    rhs = [jnp.concatenate([vs[i] * betas[i], kbs[i] * egams[i]], axis=1) for i in idx]
    sols = _solve_unit_lower(a_list, rhs)
    aqks = [_dot_nt(qs[i], ks[i]) * dmats[i] for i in idx]
    glasts = [x[BLK - 1:BLK, :] for x in gam_cs]
    ktails = [ks[i] * jnp.exp(glasts[i] - gam_cs[i]) for i in idx]
    states = [s_ref[i] for i in idx]
    wss = [_dot(jnp.concatenate([sols[i][:, GDN_D:], qs[i] * egams[i]], axis=0), states[i]) for i in idx]
    vnews = [sols[i][:, :GDN_D] - wss[i][:BLK] for i in idx]
    outs = [wss[i][BLK:] + _dot(aqks[i], vnews[i]) for i in idx]
    for i in idx:
        s_ref[i] = states[i] * jnp.exp(glasts[i]) + _dot_tn(ktails[i], vnews[i])
    for i, (g, h) in enumerate(items):
        gt = gate_ref[g, :, sls[i]].astype(F32)
        o_ref[g, :, sls[i]] = (_rmsnorm(outs[i], nw) * _silu(gt)).astype(o_ref.dtype)


def _gdn(u, small, small_t, bsz, nb, conv_w, a_log, dt_bias, norm_w):
    t = u.shape[0]
    n = GDN_HEADS * GDN_D
    grp = GDN_GROUP
    assert bsz % grp == 0
    u3 = u.reshape(bsz, nb * BLK, U_WIDTH)
    small3 = small.reshape(bsz, nb * BLK, SMALL_W)

    def lane_vec(v, off):
        return jnp.zeros((1, SMALL_W), F32).at[0, off:off + v.shape[0]].set(v.astype(F32))

    def ublk(col):
        return pl.BlockSpec((grp, BLK, n), lambda b, c: (b, c, col // n))

    def full(shape):
        return pl.BlockSpec(shape, lambda b, c: (0,) * len(shape))

    def smallt_spec(g):
        return pl.BlockSpec((SMALL_ROWS, BLK), lambda b, c: (0, (b * grp + g) * nb + c))

    y = pl.pallas_call(
        _gdn_kernel,
        grid=(bsz // grp, nb),
        in_specs=[
            ublk(COL_GDN_Q), ublk(COL_GDN_K), ublk(COL_GDN_V), ublk(COL_GDN_GATE),
            pl.BlockSpec((grp, BLK, SMALL_W), lambda b, c: (b, c, 0)),
            *[smallt_spec(g) for g in range(grp)],
            full((3 * BLK, HALO + BLK)), full((4, 3 * n)), full((1, SMALL_W)), full((1, SMALL_W)),
            full((GDN_HEADS, 1)), full((GDN_HEADS, 1)), full((1, GDN_D)),
        ],
        out_specs=pl.BlockSpec((grp, BLK, n), lambda b, c: (b, c, 0)),
        out_shape=jax.ShapeDtypeStruct((bsz, nb * BLK, n), BF16),
        scratch_shapes=[pltpu.VMEM((grp, HALO, 3 * n), U_DTYPE), pltpu.VMEM((grp * GDN_HEADS, GDN_D, GDN_D), F32)],
        compiler_params=_cparams(("parallel", "arbitrary")),
        name="gdn_mixer",
    )(u3, u3, u3, u3, small3, *([small_t] * grp), _shift_select(U_DTYPE), conv_w.astype(F32), lane_vec(a_log, 8),
      lane_vec(dt_bias, 8), a_log.astype(F32).reshape(GDN_HEADS, 1), dt_bias.astype(F32).reshape(GDN_HEADS, 1),
      norm_w.astype(F32).reshape(1, GDN_D))
    return y.reshape(t, n)


def _ssd_kernel(z_ref, x_ref, bc_ref, sm_ref, *rest):
    smt_refs = rest[:SSD_GROUP]
    (sel_ref, cw_ref, cb_ref, alog_l_ref, dtb_l_ref, alog_c_ref, dtb_c_ref, d_ref, nw_ref, e_ref, o_ref, halo_ref,
     st_ref) = rest[SSD_GROUP:]
    c = pl.program_id(1)

    @pl.when(c == 0)
    def _():
        halo_ref[...] = jnp.zeros_like(halo_ref)
        st_ref[...] = jnp.zeros_like(st_ref)

    n = SSD_HEADS * SSD_P
    gn = SSD_GROUPS * SSD_N
    gw = n // SSD_GROUPS
    hpg = SSD_HEADS // SSD_GROUPS
    cw = cw_ref[...]
    cb = cb_ref[...]
    sel = sel_ref[...]
    e = e_ref[...]
    ri = lax.broadcasted_iota(jnp.int32, (BLK, BLK), 0)
    ci = lax.broadcasted_iota(jnp.int32, (BLK, BLK), 1)
    incl = ri >= ci
    valid_rows = (c * BLK + ri) >= PAD
    ci16 = lax.broadcasted_iota(jnp.int32, (SSD_HEADS, BLK), 1)
    valid_out = (c * BLK + lax.broadcasted_iota(jnp.int32, (BLK, gw), 0)) >= PAD
    seqs = range(SSD_GROUP)

    raw = [(x_ref[s], bc_ref[s]) for s in seqs]
    halos = [halo_ref[s] for s in seqs]
    xs = [_silu(_causal_conv4(raw[s][0], halos[s][:, 0:n], cw[:, 0:n], sel) + cb[:, 0:n]) for s in seqs]
    bcs = [_silu(_causal_conv4(raw[s][1], halos[s][:, n:2 * n], cw[:, n:2 * n], sel) + cb[:, n:2 * n]) for s in seqs]
    for s in seqs:
        halo_ref[s, :, 0:n] = raw[s][0][BLK - HALO:BLK]
        halo_ref[s, :, n:2 * n] = raw[s][1][BLK - HALO:BLK]

    dtp_all = [jnp.where(valid_rows, _softplus(sm_ref[s] + dtb_l_ref[...]), 0.0) for s in seqs]
    acum_all = [_dot_sel_lhs(incl, x * (-jnp.exp(alog_l_ref[...]))) for x in dtp_all]
    dtp_x = [_dot_sel_rhs(x, e) for x in dtp_all]
    acum_x = [_dot_sel_rhs(x, e) for x in acum_all]
    dtp_rows = [jnp.where((c * BLK + ci16) >= PAD, _softplus(smt_refs[s][16:32, :] + dtb_c_ref[...]), 0.0) for s in seqs]
    acum_rows = [_dot_sel_rhs(x * (-jnp.exp(alog_c_ref[...])), ri <= ci) for x in dtp_rows]

    xdt = [xs[s] * dtp_x[s] for s in seqs]
    eacum_x = [jnp.exp(x) for x in acum_x]
    alast_x = [x[BLK - 1:BLK, :] for x in acum_x]
    xdt_dec = [xdt[s] * jnp.exp(alast_x[s] - acum_x[s]) for s in seqs]

    sg = [(s, g) for s in seqs for g in range(SSD_GROUPS)]
    sh = [(s, hh) for s in seqs for hh in range(SSD_HEADS)]
    gsl = [slice(g * gw, (g + 1) * gw) for g in range(SSD_GROUPS)]
    bgs = {(s, g): bcs[s][:, g * SSD_N:(g + 1) * SSD_N] for s, g in sg}
    cgs = {(s, g): bcs[s][:, gn + g * SSD_N:gn + (g + 1) * SSD_N] for s, g in sg}
    cbms = {k: _dot_nt(cgs[k], bgs[k]) for k in sg}
    sts = {(s, g): st_ref[s, :, gsl[g]] for s, g in sg}
    yoffs = {(s, g): _dot(cgs[s, g], sts[s, g]) * eacum_x[s][:, gsl[g]] for s, g in sg}
    lms = {(s, hh): jnp.exp(jnp.where(incl, acum_all[s][:, 16 + hh:17 + hh] - acum_rows[s][hh:hh + 1, :], -jnp.inf))
           for s, hh in sh}
    ydiag = {(s, hh): _dot(cbms[s, hh // hpg] * lms[s, hh], xdt[s][:, hh * SSD_P:(hh + 1) * SSD_P]) for s, hh in sh}
    for s, g in sg:
        st_ref[s, :, gsl[g]] = (sts[s, g] * jnp.exp(alast_x[s][:, gsl[g]])
                                + _dot_tn(bgs[s, g], xdt_dec[s][:, gsl[g]]))
    for s, g in sg:
        gs = gsl[g]
        y = yoffs[s, g] + jnp.concatenate([ydiag[s, hh] for hh in range(g * hpg, (g + 1) * hpg)], axis=1)
        y = y + d_ref[:, gs] * xs[s][:, gs]
        y = y * _silu(z_ref[s, :, gs].astype(F32))
        y = _rmsnorm(y, nw_ref[:, gs])
        o_ref[s, :, gs] = jnp.where(valid_out, y, 0.0).astype(o_ref.dtype)


def _ssd(u, small, small_t, bsz, nb, conv_w, conv_b, dt_bias, a_log, d_skip, norm_w):
    t = u.shape[0]
    n = SSD_HEADS * SSD_P
    grp = SSD_GROUP
    assert bsz % grp == 0
    u3 = u.reshape(bsz, nb * BLK, U_WIDTH)
    small3 = small.reshape(bsz, nb * BLK, SMALL_W)

    def lane_vec(v, off):
        return jnp.zeros((1, SMALL_W), F32).at[0, off:off + v.shape[0]].set(v.astype(F32))

    def ublk(col):
        return pl.BlockSpec((grp, BLK, n), lambda b, c: (b, c, col // n))

    def full(shape):
        return pl.BlockSpec(shape, lambda b, c: (0,) * len(shape))

    def smallt_spec(g):
        return pl.BlockSpec((SMALL_ROWS, BLK), lambda b, c: (0, (b * grp + g) * nb + c))

    expand = jnp.zeros((SMALL_W, n), F32).at[16:16 + SSD_HEADS].set(
        jnp.repeat(jnp.eye(SSD_HEADS, dtype=F32), SSD_P, axis=1))
    y = pl.pallas_call(
        _ssd_kernel,
        grid=(bsz // grp, nb),
        in_specs=[
            ublk(COL_SSD_Z), ublk(COL_SSD_X), ublk(COL_SSD_BC),
            pl.BlockSpec((grp, BLK, SMALL_W), lambda b, c: (b, c, 0)),
            *[smallt_spec(g) for g in range(grp)],
            full((3 * BLK, HALO + BLK)), full((4, 2 * n)), full((1, 2 * n)), full((1, SMALL_W)), full((1, SMALL_W)),
            full((SSD_HEADS, 1)), full((SSD_HEADS, 1)), full((1, n)), full((1, n)), full((SMALL_W, n)),
        ],
        out_specs=pl.BlockSpec((grp, BLK, n), lambda b, c: (b, c, 0)),
        out_shape=jax.ShapeDtypeStruct((bsz, nb * BLK, n), BF16),
        scratch_shapes=[pltpu.VMEM((grp, HALO, 2 * n), U_DTYPE), pltpu.VMEM((grp, SSD_N, n), F32)],
        compiler_params=_cparams(("parallel", "arbitrary")),
        name="ssd_mixer",
    )(u3, u3, u3, small3, *([small_t] * grp), _shift_select(U_DTYPE), conv_w.astype(F32),
      conv_b.astype(F32).reshape(1, 2 * n), lane_vec(a_log, 16), lane_vec(dt_bias, 16),
      a_log.astype(F32).reshape(SSD_HEADS, 1), dt_bias.astype(F32).reshape(SSD_HEADS, 1),
      jnp.repeat(d_skip.astype(F32), SSD_P).reshape(1, n), norm_w.astype(F32).reshape(1, n), expand)
    return y.reshape(t, n)


def _swa_kernel(sink_ref, q_ref, kc_ref, kp_ref, km_ref, vc_ref, vp_ref, vm_ref, o_ref):
    c = pl.program_id(1)
    ri = lax.broadcasted_iota(jnp.int32, (BLK, BLK), 0)
    ci = lax.broadcasted_iota(jnp.int32, (BLK, BLK), 1)
    cur_ok = (ci <= ri) & ((c > 0) | (ci >= PAD))
    prev_ok = ((c >= 2) & (ci > ri)) | ((c == 1) & (ci >= PAD))
    rep = SWA_Q_HEADS // SWA_KV_HEADS
    neg = -jnp.inf
    bias_pc = jnp.concatenate([jnp.where(prev_ok, 0.0, neg), jnp.where(cur_ok, 0.0, neg)], axis=1)
    bias_pc = jnp.concatenate([bias_pc] * rep, axis=0)
    bias_m = jnp.where(c >= 2, 0.0, neg)
    lo = ci < SWA_DH
    lo_m = lax.broadcasted_iota(jnp.int32, (N_META, BLK), 1) < SWA_DH
    row4 = lax.broadcasted_iota(jnp.int32, (rep * BLK, 1), 0)

    def both_halves(ref, hk, lo_mask):
        x = ref[:, (hk // 2) * BLK:(hk // 2 + 1) * BLK].astype(F32)
        x = jnp.where(lo_mask, x, 0.0) if hk % 2 == 0 else jnp.where(lo_mask, 0.0, x)
        return x + pltpu.roll(x, SWA_DH, axis=1)

    def stacked_queries(hk):
        scale = SWA_DH ** -0.5
        qa = q_ref[:, (2 * hk) * BLK:(2 * hk + 1) * BLK].astype(F32) * scale
        qb = q_ref[:, (2 * hk + 1) * BLK:(2 * hk + 2) * BLK].astype(F32) * scale
        return jnp.concatenate([jnp.where(lo, qa, 0.0), jnp.where(lo, 0.0, qa),
                                jnp.where(lo, qb, 0.0), jnp.where(lo, 0.0, qb)], axis=0)

    kvh = range(SWA_KV_HEADS)
    qs = [stacked_queries(hk) for hk in kvh]
    k_pc = [jnp.concatenate([both_halves(kp_ref, hk, lo), both_halves(kc_ref, hk, lo)], axis=0) for hk in kvh]
    k_m = [both_halves(km_ref, hk, lo_m) for hk in kvh]
    v_pc = [jnp.concatenate([both_halves(vp_ref, hk, lo), both_halves(vc_ref, hk, lo)], axis=0) for hk in kvh]
    v_m = [both_halves(vm_ref, hk, lo_m) for hk in kvh]
    s_pc = [_dot_nt(qs[hk], k_pc[hk]) + bias_pc for hk in kvh]
    s_m = [_dot_nt(qs[hk], k_m[hk]) + bias_m for hk in kvh]
    def sink_column(hk):
        col = jnp.full((rep * BLK, 1), sink_ref[hk * rep + rep - 1], F32)
        for r in range(rep - 2, -1, -1):
            col = jnp.where(row4 < (r + 1) * BLK, sink_ref[hk * rep + r], col)
        return col

    sinks = [sink_column(hk) for hk in kvh]
    ms = [jnp.maximum(jnp.maximum(jnp.max(s_pc[hk], axis=-1, keepdims=True), jnp.max(s_m[hk], axis=-1, keepdims=True)),
                      sinks[hk]) for hk in kvh]
    e_pc = [jnp.exp(s_pc[hk] - ms[hk]) for hk in kvh]
    e_m = [jnp.exp(s_m[hk] - ms[hk]) for hk in kvh]
    invs = [1.0 / (jnp.sum(e_pc[hk], axis=-1, keepdims=True) + jnp.sum(e_m[hk], axis=-1, keepdims=True)
                   + jnp.exp(sinks[hk] - ms[hk])) for hk in kvh]
    outs = [_dot(e_pc[hk] * invs[hk], v_pc[hk]) + _dot(e_m[hk] * invs[hk], v_m[hk]) for hk in kvh]
    for hk in kvh:
        o = outs[hk]
        o_ref[:, (2 * hk) * BLK:(2 * hk + 1) * BLK] = jnp.where(lo, o[0:BLK], o[BLK:2 * BLK]).astype(o_ref.dtype)
        o_ref[:, (2 * hk + 1) * BLK:(2 * hk + 2) * BLK] = jnp.where(lo, o[2 * BLK:3 * BLK],
                                                                     o[3 * BLK:]).astype(o_ref.dtype)


def _swa(u, bsz, nb, sinks):
    t = u.shape[0]
    nq = SWA_Q_HEADS * SWA_DH
    nkv = SWA_KV_HEADS * SWA_DH
    meta_blk = PAD // N_META
    rows16 = (nb * BLK) // N_META

    def cur(col):
        return pl.BlockSpec((BLK, nkv), lambda b, c: (b * nb + c, col // nkv))

    def prev(col):
        return pl.BlockSpec((BLK, nkv), lambda b, c: (b * nb + jnp.maximum(c - 1, 0), col // nkv))

    def meta(col):
        return pl.BlockSpec((N_META, nkv), lambda b, c: (b * rows16 + meta_blk, col // nkv))

    return pl.pallas_call(
        _swa_kernel,
        grid=(bsz, nb),
        in_specs=[
            pl.BlockSpec(memory_space=pltpu.SMEM),
            pl.BlockSpec((BLK, nq), lambda b, c: (b * nb + c, COL_SWA_Q // nq)),
            cur(COL_SWA_K), prev(COL_SWA_K), meta(COL_SWA_K),
            cur(COL_SWA_V), prev(COL_SWA_V), meta(COL_SWA_V),
        ],
        out_specs=pl.BlockSpec((BLK, nq), lambda b, c: (b * nb + c, 0)),
        out_shape=jax.ShapeDtypeStruct((t, nq), BF16),
        compiler_params=_cparams(("parallel", "parallel")),
        name="swa_mixer",
    )(sinks.astype(F32), u, u, u, u, u, u, u)


def _merge_kernel(h_ref, yg_ref, ys_ref, yc_ref, ga_ref, gb_ref, gc_ref, wg_ref, ws_ref, wc_ref, wo_ref, o_ref):
    d = functools.partial(jnp.dot, preferred_element_type=F32)
    merged = (_sigmoid(ga_ref[...].astype(F32)) * d(yg_ref[...], wg_ref[...])
              + _sigmoid(gb_ref[...].astype(F32)) * d(ys_ref[...], ws_ref[...])
              + _sigmoid(gc_ref[...].astype(F32)) * d(yc_ref[...], wc_ref[...]))
    o_ref[...] = h_ref[...] + d(merged.astype(BF16), wo_ref[...])


def _merge(h, y_gdn, y_ssd, y_swa, u, w_gdn, w_ssd, w_swa, w_out):
    t = h.shape[0]
    tm = _pick_tile(t, (512, 384, 256, 128))
    row = pl.BlockSpec((tm, D_MODEL), lambda i: (i, 0))
    wfull = pl.BlockSpec((D_MODEL, D_MODEL), lambda i: (0, 0))

    def gate(k):
        return pl.BlockSpec((tm, D_MODEL), lambda i: (i, COL_GATES // D_MODEL + k))

    return pl.pallas_call(
        _merge_kernel,
        grid=(t // tm,),
        in_specs=[row, row, row, row, gate(0), gate(1), gate(2), wfull, wfull, wfull, wfull],
        out_specs=row,
        out_shape=jax.ShapeDtypeStruct((t, D_MODEL), F32),
        compiler_params=_cparams(("parallel",)),
        name="merge",
    )(h, y_gdn, y_ssd, y_swa, u, u, u, w_gdn, w_ssd, w_swa, w_out)


def _mlp_kernel(h_ref, nw_ref, wu_ref, wd_ref, fw_ref, o_ref, hn_ref, acc_ref, *, final_norm):
    j = pl.program_id(1)

    @pl.when(j == 0)
    def _():
        hn_ref[...] = _rmsnorm(h_ref[...], nw_ref[...]).astype(BF16)
        acc_ref[...] = jnp.zeros_like(acc_ref)

    up = jnp.maximum(jnp.dot(hn_ref[...], wu_ref[...], preferred_element_type=F32), 0.0)
    acc_ref[...] += jnp.dot((up * up).astype(BF16), wd_ref[...], preferred_element_type=F32)

    @pl.when(j == pl.num_programs(1) - 1)
    def _():
        r = h_ref[...] + acc_ref[...]
        if final_norm:
            r = _rmsnorm(r, fw_ref[...])
        o_ref[...] = r


def _mlp(h, norm_w, w_up, w_down, final_w, final_norm):
    t = h.shape[0]
    tm = _pick_tile(t, (1024, 768, 512, 384, 256, 128))
    tf = 1024
    return pl.pallas_call(
        functools.partial(_mlp_kernel, final_norm=final_norm),
        grid=(t // tm, D_FF // tf),
        in_specs=[
            pl.BlockSpec((tm, D_MODEL), lambda i, j: (i, 0)),
            pl.BlockSpec((1, D_MODEL), lambda i, j: (0, 0)),
            pl.BlockSpec((D_MODEL, tf), lambda i, j: (0, j)),
            pl.BlockSpec((tf, D_MODEL), lambda i, j: (j, 0)),
            pl.BlockSpec((1, D_MODEL), lambda i, j: (0, 0)),
        ],
        out_specs=pl.BlockSpec((tm, D_MODEL), lambda i, j: (i, 0)),
        out_shape=jax.ShapeDtypeStruct((t, D_MODEL), F32),
        scratch_shapes=[pltpu.VMEM((tm, D_MODEL), BF16), pltpu.VMEM((tm, D_MODEL), F32)],
        compiler_params=_cparams(("parallel", "arbitrary")),
        name="mlp",
    )(h, norm_w, w_up, w_down, final_w)


def _permute_w_in(w):
    a_small = 4 * GDN_HEADS * GDN_D
    b_z = a_small + 2 * GDN_HEADS
    b_dt = b_z + 3 * SSD_HEADS * SSD_P
    c_q = b_dt + SSD_HEADS
    c_k = c_q + SWA_Q_HEADS * SWA_DH
    c_v = c_k + SWA_KV_HEADS * SWA_DH
    gates = c_v + SWA_KV_HEADS * SWA_DH
    main = jnp.concatenate([w[:, :a_small], w[:, b_z:b_dt], w[:, c_q:c_k], w[:, gates:], w[:, c_k:c_v],
                            w[:, c_v:gates]], axis=1)
    small = jnp.concatenate([w[:, a_small:b_z], w[:, b_dt:c_q],
                             jnp.zeros((w.shape[0], SMALL_W - SMALL_ROWS), w.dtype)], axis=1)
    return main.astype(BF16), small.astype(BF16)


def kernel(x, meta_tokens, norm1_w, w_in, gdn_conv_w, gdn_a_log, gdn_dt_bias, gdn_norm_w, ssd_conv_w, ssd_conv_b,
           ssd_dt_bias, ssd_a_log, ssd_d, ssd_norm_w, swa_sinks, w_proj_gdn, w_proj_ssd, w_proj_swa, w_out, norm2_w,
           w_up, w_down, final_norm_w):
    bsz, seq, dm = x.shape
    depth = w_in.shape[0]
    assert dm == D_MODEL and seq % BLK == 0
    nb = seq // BLK + 1
    lp = nb * BLK
    meta = jnp.broadcast_to(meta_tokens.astype(x.dtype)[None], (bsz, N_META, dm))
    h = jnp.concatenate([jnp.zeros((bsz, PAD, dm), x.dtype), meta, x], axis=1).reshape(bsz * lp, dm)
    final_w = final_norm_w.astype(F32).reshape(1, dm)
    for l in range(depth):
        w_main, w_small = _permute_w_in(w_in[l])
        u, small, small_t = _inproj(h, norm1_w[l].astype(F32).reshape(1, dm), w_main, w_small)
        y_gdn = _gdn(u, small, small_t, bsz, nb, gdn_conv_w[l], gdn_a_log[l], gdn_dt_bias[l], gdn_norm_w[l])
        y_ssd = _ssd(u, small, small_t, bsz, nb, ssd_conv_w[l], ssd_conv_b[l], ssd_dt_bias[l], ssd_a_log[l],
                     ssd_d[l], ssd_norm_w[l])
        y_swa = _swa(u, bsz, nb, swa_sinks[l])
        h = _merge(h, y_gdn, y_ssd, y_swa, u, w_proj_gdn[l].astype(BF16), w_proj_ssd[l].astype(BF16),
                   w_proj_swa[l].astype(BF16), w_out[l].astype(BF16))
        h = _mlp(h, norm2_w[l].astype(F32).reshape(1, dm), w_up[l].astype(BF16), w_down[l].astype(BF16), final_w,
                 final_norm=(l == depth - 1))
    return h.reshape(bsz, lp, dm)[:, BLK:]
```

```python
import functools

import jax
import jax.numpy as jnp
from jax import lax
from jax.experimental import pallas as pl
from jax.experimental.pallas import tpu as pltpu

F32 = jnp.float32
BF16 = jnp.bfloat16

D_MODEL = 1024
D_FF = 4 * D_MODEL
N_META = 16
RMS_EPS = 1e-6
L2_EPS = 1e-6

BLK = 128
PAD = BLK - N_META
HALO = 16

GDN_HEADS = 8
GDN_D = 128
GDN_GROUP = 2
SSD_HEADS = 16
SSD_P = 64
SSD_GROUPS = 4
SSD_N = 128
SSD_GROUP = 1
SWA_Q_HEADS = 16
SWA_KV_HEADS = 4
SWA_DH = 64

COL_GDN_Q, COL_GDN_K, COL_GDN_V, COL_GDN_GATE = 0, 1024, 2048, 3072
COL_SSD_Z, COL_SSD_X, COL_SSD_BC = 4096, 5120, 6144
COL_SWA_Q = 7168
COL_GATES = 8192
COL_SWA_K, COL_SWA_V = 11264, 11520
U_WIDTH = 11776
SMALL_W = 128
SMALL_ROWS = 32
U_DTYPE = BF16

VMEM_LIMIT = 52 * 1024 * 1024


def _cparams(sem):
    return pltpu.CompilerParams(dimension_semantics=sem, vmem_limit_bytes=VMEM_LIMIT)


def _pick_tile(n, candidates):
    for c in candidates:
        if n % c == 0:
            return c
    raise ValueError(f"no tile for {n}")


def _dot(a, b):
    return jnp.dot(a.astype(BF16), b.astype(BF16), preferred_element_type=F32)


def _dot_nt(a, b):
    return lax.dot_general(a.astype(BF16), b.astype(BF16), (((1,), (1,)), ((), ())), preferred_element_type=F32)


def _dot_tn(a, b):
    return lax.dot_general(a.astype(BF16), b.astype(BF16), (((0,), (0,)), ((), ())), preferred_element_type=F32)


def _split3_bf16(a):
    p1 = a.astype(BF16)
    r1 = a - p1.astype(F32)
    p2 = r1.astype(BF16)
    p3 = (r1 - p2.astype(F32)).astype(BF16)
    return p1, p2, p3


def _dot_sel_rhs(a, sel):
    s = sel.astype(F32).astype(BF16)
    return sum(jnp.dot(p, s, preferred_element_type=F32) for p in _split3_bf16(a))


def _dot_sel_lhs(sel, b):
    s = sel.astype(F32).astype(BF16)
    return sum(jnp.dot(s, p, preferred_element_type=F32) for p in _split3_bf16(b))


def _sigmoid(x):
    return 0.5 * jnp.tanh(0.5 * x) + 0.5


def _silu(x):
    h = 0.5 * x
    return h * jnp.tanh(h) + h


def _softplus(x):
    return jnp.maximum(x, 0.0) + jnp.log1p(jnp.exp(-jnp.abs(x)))


def _rmsnorm(x, w):
    return x * lax.rsqrt(jnp.mean(x * x, axis=-1, keepdims=True) + RMS_EPS) * w


def _shift_select(dtype):
    r = jnp.arange(3 * BLK)[:, None]
    m = jnp.arange(HALO + BLK)[None, :]
    return (m == HALO + r % BLK - 3 + r // BLK).astype(dtype)


def _causal_conv4(x, halo, w, sel):
    c = x.shape[0]
    z = jnp.dot(sel, jnp.concatenate([halo, x], axis=0), preferred_element_type=F32)
    return x.astype(F32) * w[3:4] + z[0:c] * w[0:1] + z[c:2 * c] * w[1:2] + z[2 * c:] * w[2:3]


SOLVE_BASE = 16


def _solve_unit_lower(a_list, x_list):
    c = a_list[0].shape[0]
    ri = lax.broadcasted_iota(jnp.int32, (c, c), 0)
    ci = lax.broadcasted_iota(jnp.int32, (c, c), 1)

    def blk(x, size):
        return lax.shift_right_logical(x, size.bit_length() - 1)

    diag = blk(ri, SOLVE_BASE) == blk(ci, SOLVE_BASE)
    ps = [jnp.where(diag, a, 0.0) for a in a_list]
    ts = [jnp.where(ri == ci, 1.0, 0.0) - p for p in ps]
    size = 2
    while size < SOLVE_BASE:
        ps = [_dot(p, p) for p in ps]
        ts = [t + _dot(p, t) for p, t in zip(ps, ts)]
        size *= 2
    size = SOLVE_BASE
    while size < c:
        off = (blk(ri, 2 * size) == blk(ci, 2 * size)) & (blk(ri, size) != blk(ci, size))
        ots = [_dot(jnp.where(off, a, 0.0), t) for a, t in zip(a_list, ts)]
        ts = [t - _dot(t, ot) for t, ot in zip(ts, ots)]
        size *= 2
    return [_dot(t, x) for t, x in zip(ts, x_list)]


def _inproj_kernel(h_ref, nw_ref, w_ref, ws_ref, u_ref, small_ref, smallt_ref, hn_ref):
    @pl.when(pl.program_id(1) == 0)
    def _():
        hn = _rmsnorm(h_ref[...], nw_ref[...]).astype(BF16)
        hn_ref[...] = hn
        small = jnp.dot(hn, ws_ref[...], preferred_element_type=F32)
        small_ref[...] = small
        smallt_ref[...] = small.T[:SMALL_ROWS]

    u_ref[...] = jnp.dot(hn_ref[...], w_ref[...], preferred_element_type=F32).astype(u_ref.dtype)


def _inproj(h, norm_w, w_main, w_small):
    t = h.shape[0]
    tm = _pick_tile(t, (3072, 1536, 1024, 768, 512, 384, 256, 128))
    tn = 512
    return pl.pallas_call(
        _inproj_kernel,
        grid=(t // tm, U_WIDTH // tn),
        in_specs=[
            pl.BlockSpec((tm, D_MODEL), lambda i, j: (i, 0)),
            pl.BlockSpec((1, D_MODEL), lambda i, j: (0, 0)),
            pl.BlockSpec((D_MODEL, tn), lambda i, j: (0, j)),
            pl.BlockSpec((D_MODEL, SMALL_W), lambda i, j: (0, 0)),
        ],
        out_specs=[
            pl.BlockSpec((tm, tn), lambda i, j: (i, j)),
            pl.BlockSpec((tm, SMALL_W), lambda i, j: (i, 0)),
            pl.BlockSpec((SMALL_ROWS, tm), lambda i, j: (0, i)),
        ],
        out_shape=[
            jax.ShapeDtypeStruct((t, U_WIDTH), U_DTYPE),
            jax.ShapeDtypeStruct((t, SMALL_W), F32),
            jax.ShapeDtypeStruct((SMALL_ROWS, t), F32),
        ],
        scratch_shapes=[pltpu.VMEM((tm, D_MODEL), BF16)],
        compiler_params=_cparams(("parallel", "arbitrary")),
        name="inproj",
    )(h, norm_w, w_main, w_small)


def _gdn_kernel(q_ref, k_ref, v_ref, gate_ref, sm_ref, *rest):
    smt_refs = rest[:GDN_GROUP]
    sel_ref, cw_ref, alog_l_ref, dtb_l_ref, alog_c_ref, dtb_c_ref, nw_ref, o_ref, halo_ref, s_ref = rest[GDN_GROUP:]
    c = pl.program_id(1)

    @pl.when(c == 0)
    def _():
        halo_ref[...] = jnp.zeros_like(halo_ref)
        s_ref[...] = jnp.zeros_like(s_ref)

    n = GDN_HEADS * GDN_D
    cw = cw_ref[...]
    sel = sel_ref[...]
    ri = lax.broadcasted_iota(jnp.int32, (BLK, BLK), 0)
    ci = lax.broadcasted_iota(jnp.int32, (BLK, BLK), 1)
    incl = ri >= ci
    strict = ri > ci
    valid_rows = (c * BLK + ri) >= PAD
    ci8 = lax.broadcasted_iota(jnp.int32, (GDN_HEADS, BLK), 1)
    nw = nw_ref[...]
    seqs = range(GDN_GROUP)

    raw = [(q_ref[g], k_ref[g], v_ref[g]) for g in seqs]
    halos = [halo_ref[g] for g in seqs]
    conv = [[_silu(_causal_conv4(raw[g][j], halos[g][:, j * n:(j + 1) * n], cw[:, j * n:(j + 1) * n], sel))
             for j in range(3)] for g in seqs]
    for g in seqs:
        for j in range(3):
            halo_ref[g, :, j * n:(j + 1) * n] = raw[g][j][BLK - HALO:BLK]

    sms = [sm_ref[g] for g in seqs]
    beta_all = [jnp.where(valid_rows, _sigmoid(sm), 0.0) for sm in sms]
    g_all = [jnp.where(valid_rows, -jnp.exp(alog_l_ref[...]) * _softplus(sm + dtb_l_ref[...]), 0.0) for sm in sms]
    gam_all = [_dot_sel_lhs(incl, x) for x in g_all]
    g_rows = [jnp.where((c * BLK + ci8) >= PAD,
                        -jnp.exp(alog_c_ref[...]) * _softplus(smt_refs[g][8:16, :] + dtb_c_ref[...]), 0.0)
              for g in seqs]
    gam_rows = [_dot_sel_rhs(x, ri <= ci) for x in g_rows]

    items = [(g, h) for g in seqs for h in range(GDN_HEADS)]
    idx = range(len(items))
    sls = [slice(h * GDN_D, (h + 1) * GDN_D) for _, h in items]
    qs = [conv[g][0][:, sls[i]] for i, (g, h) in enumerate(items)]
    ks = [conv[g][1][:, sls[i]] for i, (g, h) in enumerate(items)]
    vs = [conv[g][2][:, sls[i]] for i, (g, h) in enumerate(items)]
    qs = [q * lax.rsqrt(jnp.sum(q * q, axis=-1, keepdims=True) + L2_EPS) * (GDN_D ** -0.5) for q in qs]
    ks = [k * lax.rsqrt(jnp.sum(k * k, axis=-1, keepdims=True) + L2_EPS) for k in ks]
    betas = [beta_all[g][:, h:h + 1] for g, h in items]
    gam_cs = [gam_all[g][:, 8 + h:9 + h] for g, h in items]
    dmats = [jnp.exp(jnp.where(incl, gam_cs[i] - gam_rows[g][h:h + 1, :], -jnp.inf)) for i, (g, h) in enumerate(items)]
    kbs = [ks[i] * betas[i] for i in idx]
    egams = [jnp.exp(x) for x in gam_cs]
    a_list = [jnp.where(strict, _dot_nt(kbs[i], ks[i]) * dmats[i], 0.0) for i in idx]
    rhs = [jnp.concatenate([vs[i] * betas[i], kbs[i] * egams[i]], axis=1) for i in idx]
    sols = _solve_unit_lower(a_list, rhs)
    aqks = [_dot_nt(qs[i], ks[i]) * dmats[i] for i in idx]
    glasts = [x[BLK - 1:BLK, :] for x in gam_cs]
    ktails = [ks[i] * jnp.exp(glasts[i] - gam_cs[i]) for i in idx]
    states = [s_ref[i] for i in idx]
    wss = [_dot(jnp.concatenate([sols[i][:, GDN_D:], qs[i] * egams[i]], axis=0), states[i]) for i in idx]
    vnews = [sols[i][:, :GDN_D] - wss[i][:BLK] for i in idx]
    outs = [wss[i][BLK:] + _dot(aqks[i], vnews[i]) for i in idx]
    for i in idx:
        s_ref[i] = states[i] * jnp.exp(glasts[i]) + _dot_tn(ktails[i], vnews[i])
    for i, (g, h) in enumerate(items):
        gt = gate_ref[g, :, sls[i]].astype(F32)
        o_ref[g, :, sls[i]] = (_rmsnorm(outs[i], nw) * _silu(gt)).astype(o_ref.dtype)


def _gdn(u, small, small_t, bsz, nb, conv_w, a_log, dt_bias, norm_w):
    t = u.shape[0]
    n = GDN_HEADS * GDN_D
    grp = GDN_GROUP
    assert bsz % grp == 0
    u3 = u.reshape(bsz, nb * BLK, U_WIDTH)
    small3 = small.reshape(bsz, nb * BLK, SMALL_W)

    def lane_vec(v, off):
        return jnp.zeros((1, SMALL_W), F32).at[0, off:off + v.shape[0]].set(v.astype(F32))

    def ublk(col):
        return pl.BlockSpec((grp, BLK, n), lambda b, c: (b, c, col // n))

    def full(shape):
        return pl.BlockSpec(shape, lambda b, c: (0,) * len(shape))

    def smallt_spec(g):
        return pl.BlockSpec((SMALL_ROWS, BLK), lambda b, c: (0, (b * grp + g) * nb + c))

    y = pl.pallas_call(
        _gdn_kernel,
        grid=(bsz // grp, nb),
        in_specs=[
            ublk(COL_GDN_Q), ublk(COL_GDN_K), ublk(COL_GDN_V), ublk(COL_GDN_GATE),
            pl.BlockSpec((grp, BLK, SMALL_W), lambda b, c: (b, c, 0)),
            *[smallt_spec(g) for g in range(grp)],
            full((3 * BLK, HALO + BLK)), full((4, 3 * n)), full((1, SMALL_W)), full((1, SMALL_W)),
            full((GDN_HEADS, 1)), full((GDN_HEADS, 1)), full((1, GDN_D)),
        ],
        out_specs=pl.BlockSpec((grp, BLK, n), lambda b, c: (b, c, 0)),
        out_shape=jax.ShapeDtypeStruct((bsz, nb * BLK, n), BF16),
        scratch_shapes=[pltpu.VMEM((grp, HALO, 3 * n), U_DTYPE), pltpu.VMEM((grp * GDN_HEADS, GDN_D, GDN_D), F32)],
        compiler_params=_cparams(("parallel", "arbitrary")),
        name="gdn_mixer",
    )(u3, u3, u3, u3, small3, *([small_t] * grp), _shift_select(U_DTYPE), conv_w.astype(F32), lane_vec(a_log, 8),
      lane_vec(dt_bias, 8), a_log.astype(F32).reshape(GDN_HEADS, 1), dt_bias.astype(F32).reshape(GDN_HEADS, 1),
      norm_w.astype(F32).reshape(1, GDN_D))
    return y.reshape(t, n)


def _ssd_kernel(z_ref, x_ref, bc_ref, sm_ref, *rest):
    smt_refs = rest[:SSD_GROUP]
    (sel_ref, cw_ref, cb_ref, alog_l_ref, dtb_l_ref, alog_c_ref, dtb_c_ref, d_ref, nw_ref, e_ref, o_ref, halo_ref,
     st_ref) = rest[SSD_GROUP:]
    c = pl.program_id(1)

    @pl.when(c == 0)
    def _():
        halo_ref[...] = jnp.zeros_like(halo_ref)
        st_ref[...] = jnp.zeros_like(st_ref)

    n = SSD_HEADS * SSD_P
    gn = SSD_GROUPS * SSD_N
    gw = n // SSD_GROUPS
    hpg = SSD_HEADS // SSD_GROUPS
    cw = cw_ref[...]
    cb = cb_ref[...]
    sel = sel_ref[...]
    e = e_ref[...]
    ri = lax.broadcasted_iota(jnp.int32, (BLK, BLK), 0)
    ci = lax.broadcasted_iota(jnp.int32, (BLK, BLK), 1)
    incl = ri >= ci
    valid_rows = (c * BLK + ri) >= PAD
    ci16 = lax.broadcasted_iota(jnp.int32, (SSD_HEADS, BLK), 1)
    valid_out = (c * BLK + lax.broadcasted_iota(jnp.int32, (BLK, gw), 0)) >= PAD
    seqs = range(SSD_GROUP)

    raw = [(x_ref[s], bc_ref[s]) for s in seqs]
    halos = [halo_ref[s] for s in seqs]
    xs = [_silu(_causal_conv4(raw[s][0], halos[s][:, 0:n], cw[:, 0:n], sel) + cb[:, 0:n]) for s in seqs]
    bcs = [_silu(_causal_conv4(raw[s][1], halos[s][:, n:2 * n], cw[:, n:2 * n], sel) + cb[:, n:2 * n]) for s in seqs]
    for s in seqs:
        halo_ref[s, :, 0:n] = raw[s][0][BLK - HALO:BLK]
        halo_ref[s, :, n:2 * n] = raw[s][1][BLK - HALO:BLK]

    dtp_all = [jnp.where(valid_rows, _softplus(sm_ref[s] + dtb_l_ref[...]), 0.0) for s in seqs]
    acum_all = [_dot_sel_lhs(incl, x * (-jnp.exp(alog_l_ref[...]))) for x in dtp_all]
    dtp_x = [_dot_sel_rhs(x, e) for x in dtp_all]
    acum_x = [_dot_sel_rhs(x, e) for x in acum_all]
    dtp_rows = [jnp.where((c * BLK + ci16) >= PAD, _softplus(smt_refs[s][16:32, :] + dtb_c_ref[...]), 0.0) for s in seqs]
    acum_rows = [_dot_sel_rhs(x * (-jnp.exp(alog_c_ref[...])), ri <= ci) for x in dtp_rows]

    xdt = [xs[s] * dtp_x[s] for s in seqs]
    eacum_x = [jnp.exp(x) for x in acum_x]
    alast_x = [x[BLK - 1:BLK, :] for x in acum_x]
    xdt_dec = [xdt[s] * jnp.exp(alast_x[s] - acum_x[s]) for s in seqs]

    sg = [(s, g) for s in seqs for g in range(SSD_GROUPS)]
    sh = [(s, hh) for s in seqs for hh in range(SSD_HEADS)]
    gsl = [slice(g * gw, (g + 1) * gw) for g in range(SSD_GROUPS)]
    bgs = {(s, g): bcs[s][:, g * SSD_N:(g + 1) * SSD_N] for s, g in sg}
    cgs = {(s, g): bcs[s][:, gn + g * SSD_N:gn + (g + 1) * SSD_N] for s, g in sg}
    cbms = {k: _dot_nt(cgs[k], bgs[k]) for k in sg}
    sts = {(s, g): st_ref[s, :, gsl[g]] for s, g in sg}
    yoffs = {(s, g): _dot(cgs[s, g], sts[s, g]) * eacum_x[s][:, gsl[g]] for s, g in sg}
    lms = {(s, hh): jnp.exp(jnp.where(incl, acum_all[s][:, 16 + hh:17 + hh] - acum_rows[s][hh:hh + 1, :], -jnp.inf))
           for s, hh in sh}
    ydiag = {(s, hh): _dot(cbms[s, hh // hpg] * lms[s, hh], xdt[s][:, hh * SSD_P:(hh + 1) * SSD_P]) for s, hh in sh}
    for s, g in sg:
        st_ref[s, :, gsl[g]] = (sts[s, g] * jnp.exp(alast_x[s][:, gsl[g]])
                                + _dot_tn(bgs[s, g], xdt_dec[s][:, gsl[g]]))
    for s, g in sg:
        gs = gsl[g]
        y = yoffs[s, g] + jnp.concatenate([ydiag[s, hh] for hh in range(g * hpg, (g + 1) * hpg)], axis=1)
        y = y + d_ref[:, gs] * xs[s][:, gs]
        y = y * _silu(z_ref[s, :, gs].astype(F32))
        y = _rmsnorm(y, nw_ref[:, gs])
        o_ref[s, :, gs] = jnp.where(valid_out, y, 0.0).astype(o_ref.dtype)


def _ssd(u, small, small_t, bsz, nb, conv_w, conv_b, dt_bias, a_log, d_skip, norm_w):
    t = u.shape[0]
    n = SSD_HEADS * SSD_P
    grp = SSD_GROUP
    assert bsz % grp == 0
    u3 = u.reshape(bsz, nb * BLK, U_WIDTH)
    small3 = small.reshape(bsz, nb * BLK, SMALL_W)

    def lane_vec(v, off):
        return jnp.zeros((1, SMALL_W), F32).at[0, off:off + v.shape[0]].set(v.astype(F32))

    def ublk(col):
        return pl.BlockSpec((grp, BLK, n), lambda b, c: (b, c, col // n))

    def full(shape):
        return pl.BlockSpec(shape, lambda b, c: (0,) * len(shape))

    def smallt_spec(g):
        return pl.BlockSpec((SMALL_ROWS, BLK), lambda b, c: (0, (b * grp + g) * nb + c))

    expand = jnp.zeros((SMALL_W, n), F32).at[16:16 + SSD_HEADS].set(
        jnp.repeat(jnp.eye(SSD_HEADS, dtype=F32), SSD_P, axis=1))
    y = pl.pallas_call(
        _ssd_kernel,
        grid=(bsz // grp, nb),
        in_specs=[
            ublk(COL_SSD_Z), ublk(COL_SSD_X), ublk(COL_SSD_BC),
            pl.BlockSpec((grp, BLK, SMALL_W), lambda b, c: (b, c, 0)),
            *[smallt_spec(g) for g in range(grp)],
            full((3 * BLK, HALO + BLK)), full((4, 2 * n)), full((1, 2 * n)), full((1, SMALL_W)), full((1, SMALL_W)),
            full((SSD_HEADS, 1)), full((SSD_HEADS, 1)), full((1, n)), full((1, n)), full((SMALL_W, n)),
        ],
        out_specs=pl.BlockSpec((grp, BLK, n), lambda b, c: (b, c, 0)),
        out_shape=jax.ShapeDtypeStruct((bsz, nb * BLK, n), BF16),
        scratch_shapes=[pltpu.VMEM((grp, HALO, 2 * n), U_DTYPE), pltpu.VMEM((grp, SSD_N, n), F32)],
        compiler_params=_cparams(("parallel", "arbitrary")),
        name="ssd_mixer",
    )(u3, u3, u3, small3, *([small_t] * grp), _shift_select(U_DTYPE), conv_w.astype(F32),
      conv_b.astype(F32).reshape(1, 2 * n), lane_vec(a_log, 16), lane_vec(dt_bias, 16),
      a_log.astype(F32).reshape(SSD_HEADS, 1), dt_bias.astype(F32).reshape(SSD_HEADS, 1),
      jnp.repeat(d_skip.astype(F32), SSD_P).reshape(1, n), norm_w.astype(F32).reshape(1, n), expand)
    return y.reshape(t, n)


def _swa_kernel(sink_ref, q_ref, kc_ref, kp_ref, km_ref, vc_ref, vp_ref, vm_ref, o_ref):
    c = pl.program_id(1)
    ri = lax.broadcasted_iota(jnp.int32, (BLK, BLK), 0)
    ci = lax.broadcasted_iota(jnp.int32, (BLK, BLK), 1)
    cur_ok = (ci <= ri) & ((c > 0) | (ci >= PAD))
    prev_ok = ((c >= 2) & (ci > ri)) | ((c == 1) & (ci >= PAD))
    rep = SWA_Q_HEADS // SWA_KV_HEADS
    neg = -jnp.inf
    bias_pc = jnp.concatenate([jnp.where(prev_ok, 0.0, neg), jnp.where(cur_ok, 0.0, neg)], axis=1)
    bias_pc = jnp.concatenate([bias_pc] * rep, axis=0)
    bias_m = jnp.where(c >= 2, 0.0, neg)
    lo = ci < SWA_DH
    lo_m = lax.broadcasted_iota(jnp.int32, (N_META, BLK), 1) < SWA_DH
    row4 = lax.broadcasted_iota(jnp.int32, (rep * BLK, 1), 0)

    def both_halves(ref, hk, lo_mask):
        x = ref[:, (hk // 2) * BLK:(hk // 2 + 1) * BLK].astype(F32)
        x = jnp.where(lo_mask, x, 0.0) if hk % 2 == 0 else jnp.where(lo_mask, 0.0, x)
        return x + pltpu.roll(x, SWA_DH, axis=1)

    def stacked_queries(hk):
        scale = SWA_DH ** -0.5
        qa = q_ref[:, (2 * hk) * BLK:(2 * hk + 1) * BLK].astype(F32) * scale
        qb = q_ref[:, (2 * hk + 1) * BLK:(2 * hk + 2) * BLK].astype(F32) * scale
        return jnp.concatenate([jnp.where(lo, qa, 0.0), jnp.where(lo, 0.0, qa),
                                jnp.where(lo, qb, 0.0), jnp.where(lo, 0.0, qb)], axis=0)

    kvh = range(SWA_KV_HEADS)
    qs = [stacked_queries(hk) for hk in kvh]
    k_pc = [jnp.concatenate([both_halves(kp_ref, hk, lo), both_halves(kc_ref, hk, lo)], axis=0) for hk in kvh]
    k_m = [both_halves(km_ref, hk, lo_m) for hk in kvh]
    v_pc = [jnp.concatenate([both_halves(vp_ref, hk, lo), both_halves(vc_ref, hk, lo)], axis=0) for hk in kvh]
    v_m = [both_halves(vm_ref, hk, lo_m) for hk in kvh]
    s_pc = [_dot_nt(qs[hk], k_pc[hk]) + bias_pc for hk in kvh]
    s_m = [_dot_nt(qs[hk], k_m[hk]) + bias_m for hk in kvh]
    def sink_column(hk):
        col = jnp.full((rep * BLK, 1), sink_ref[hk * rep + rep - 1], F32)
        for r in range(rep - 2, -1, -1):
            col = jnp.where(row4 < (r + 1) * BLK, sink_ref[hk * rep + r], col)
        return col

    sinks = [sink_column(hk) for hk in kvh]
    ms = [jnp.maximum(jnp.maximum(jnp.max(s_pc[hk], axis=-1, keepdims=True), jnp.max(s_m[hk], axis=-1, keepdims=True)),
                      sinks[hk]) for hk in kvh]
    e_pc = [jnp.exp(s_pc[hk] - ms[hk]) for hk in kvh]
    e_m = [jnp.exp(s_m[hk] - ms[hk]) for hk in kvh]
    invs = [1.0 / (jnp.sum(e_pc[hk], axis=-1, keepdims=True) + jnp.sum(e_m[hk], axis=-1, keepdims=True)
                   + jnp.exp(sinks[hk] - ms[hk])) for hk in kvh]
    outs = [_dot(e_pc[hk] * invs[hk], v_pc[hk]) + _dot(e_m[hk] * invs[hk], v_m[hk]) for hk in kvh]
    for hk in kvh:
        o = outs[hk]
        o_ref[:, (2 * hk) * BLK:(2 * hk + 1) * BLK] = jnp.where(lo, o[0:BLK], o[BLK:2 * BLK]).astype(o_ref.dtype)
        o_ref[:, (2 * hk + 1) * BLK:(2 * hk + 2) * BLK] = jnp.where(lo, o[2 * BLK:3 * BLK],
                                                                     o[3 * BLK:]).astype(o_ref.dtype)


def _swa(u, bsz, nb, sinks):
    t = u.shape[0]
    nq = SWA_Q_HEADS * SWA_DH
    nkv = SWA_KV_HEADS * SWA_DH
    meta_blk = PAD // N_META
    rows16 = (nb * BLK) // N_META

    def cur(col):
        return pl.BlockSpec((BLK, nkv), lambda b, c: (b * nb + c, col // nkv))

    def prev(col):
        return pl.BlockSpec((BLK, nkv), lambda b, c: (b * nb + jnp.maximum(c - 1, 0), col // nkv))

    def meta(col):
        return pl.BlockSpec((N_META, nkv), lambda b, c: (b * rows16 + meta_blk, col // nkv))

    return pl.pallas_call(
        _swa_kernel,
        grid=(bsz, nb),
        in_specs=[
            pl.BlockSpec(memory_space=pltpu.SMEM),
            pl.BlockSpec((BLK, nq), lambda b, c: (b * nb + c, COL_SWA_Q // nq)),
            cur(COL_SWA_K), prev(COL_SWA_K), meta(COL_SWA_K),
            cur(COL_SWA_V), prev(COL_SWA_V), meta(COL_SWA_V),
        ],
        out_specs=pl.BlockSpec((BLK, nq), lambda b, c: (b * nb + c, 0)),
        out_shape=jax.ShapeDtypeStruct((t, nq), BF16),
        compiler_params=_cparams(("parallel", "parallel")),
        name="swa_mixer",
    )(sinks.astype(F32), u, u, u, u, u, u, u)


def _merge_kernel(h_ref, yg_ref, ys_ref, yc_ref, ga_ref, gb_ref, gc_ref, wg_ref, ws_ref, wc_ref, wo_ref, o_ref):
    d = functools.partial(jnp.dot, preferred_element_type=F32)
    merged = (_sigmoid(ga_ref[...].astype(F32)) * d(yg_ref[...], wg_ref[...])
              + _sigmoid(gb_ref[...].astype(F32)) * d(ys_ref[...], ws_ref[...])
              + _sigmoid(gc_ref[...].astype(F32)) * d(yc_ref[...], wc_ref[...]))
    o_ref[...] = h_ref[...] + d(merged.astype(BF16), wo_ref[...])


def _merge(h, y_gdn, y_ssd, y_swa, u, w_gdn, w_ssd, w_swa, w_out):
    t = h.shape[0]
    tm = _pick_tile(t, (512, 384, 256, 128))
    row = pl.BlockSpec((tm, D_MODEL), lambda i: (i, 0))
    wfull = pl.BlockSpec((D_MODEL, D_MODEL), lambda i: (0, 0))

    def gate(k):
        return pl.BlockSpec((tm, D_MODEL), lambda i: (i, COL_GATES // D_MODEL + k))

    return pl.pallas_call(
        _merge_kernel,
        grid=(t // tm,),
        in_specs=[row, row, row, row, gate(0), gate(1), gate(2), wfull, wfull, wfull, wfull],
        out_specs=row,
        out_shape=jax.ShapeDtypeStruct((t, D_MODEL), F32),
        compiler_params=_cparams(("parallel",)),
        name="merge",
    )(h, y_gdn, y_ssd, y_swa, u, u, u, w_gdn, w_ssd, w_swa, w_out)


def _mlp_kernel(h_ref, nw_ref, wu_ref, wd_ref, fw_ref, o_ref, hn_ref, acc_ref, *, final_norm):
    j = pl.program_id(1)

    @pl.when(j == 0)
    def _():
        hn_ref[...] = _rmsnorm(h_ref[...], nw_ref[...]).astype(BF16)
        acc_ref[...] = jnp.zeros_like(acc_ref)

    up = jnp.maximum(jnp.dot(hn_ref[...], wu_ref[...], preferred_element_type=F32), 0.0)
    acc_ref[...] += jnp.dot((up * up).astype(BF16), wd_ref[...], preferred_element_type=F32)

    @pl.when(j == pl.num_programs(1) - 1)
    def _():
        r = h_ref[...] + acc_ref[...]
        if final_norm:
            r = _rmsnorm(r, fw_ref[...])
        o_ref[...] = r


def _mlp(h, norm_w, w_up, w_down, final_w, final_norm, seq_rows=None):
    t = h.shape[0]
    tf = 1024
    if seq_rows is None:
        tm = _pick_tile(t, (1024, 768, 512, 384, 256, 128))
        n_rows = t
        h_spec = pl.BlockSpec((tm, D_MODEL), lambda i, j: (i, 0))
    else:
        bsz, lp, seq = seq_rows
        tm = _pick_tile(seq, (1024, 768, 512, 384, 256, 128))
        per_seq = seq // tm
        n_rows = bsz * seq
        h_spec = pl.BlockSpec((pl.Element(tm), pl.Element(D_MODEL)),
                              lambda i, j: (pl.multiple_of((i // per_seq) * lp + (lp - seq) + (i % per_seq) * tm, BLK), 0))
    return pl.pallas_call(
        functools.partial(_mlp_kernel, final_norm=final_norm),
        grid=(n_rows // tm, D_FF // tf),
        in_specs=[
            h_spec,
            pl.BlockSpec((1, D_MODEL), lambda i, j: (0, 0)),
            pl.BlockSpec((D_MODEL, tf), lambda i, j: (0, j)),
            pl.BlockSpec((tf, D_MODEL), lambda i, j: (j, 0)),
            pl.BlockSpec((1, D_MODEL), lambda i, j: (0, 0)),
        ],
        out_specs=pl.BlockSpec((tm, D_MODEL), lambda i, j: (i, 0)),
        out_shape=jax.ShapeDtypeStruct((n_rows, D_MODEL), F32),
        scratch_shapes=[pltpu.VMEM((tm, D_MODEL), BF16), pltpu.VMEM((tm, D_MODEL), F32)],
        compiler_params=_cparams(("parallel", "arbitrary")),
        name="mlp",
    )(h, norm_w, w_up, w_down, final_w)


def _permute_w_in(w):
    a_small = 4 * GDN_HEADS * GDN_D
    b_z = a_small + 2 * GDN_HEADS
    b_dt = b_z + 3 * SSD_HEADS * SSD_P
    c_q = b_dt + SSD_HEADS
    c_k = c_q + SWA_Q_HEADS * SWA_DH
    c_v = c_k + SWA_KV_HEADS * SWA_DH
    gates = c_v + SWA_KV_HEADS * SWA_DH
    main = jnp.concatenate([w[:, :a_small], w[:, b_z:b_dt], w[:, c_q:c_k], w[:, gates:], w[:, c_k:c_v],
                            w[:, c_v:gates]], axis=1)
    small = jnp.concatenate([w[:, a_small:b_z], w[:, b_dt:c_q],
                             jnp.zeros((w.shape[0], SMALL_W - SMALL_ROWS), w.dtype)], axis=1)
    return main.astype(BF16), small.astype(BF16)


def kernel(x, meta_tokens, norm1_w, w_in, gdn_conv_w, gdn_a_log, gdn_dt_bias, gdn_norm_w, ssd_conv_w, ssd_conv_b,
           ssd_dt_bias, ssd_a_log, ssd_d, ssd_norm_w, swa_sinks, w_proj_gdn, w_proj_ssd, w_proj_swa, w_out, norm2_w,
           w_up, w_down, final_norm_w):
    bsz, seq, dm = x.shape
    depth = w_in.shape[0]
    assert dm == D_MODEL and seq % BLK == 0
    nb = seq // BLK + 1
    lp = nb * BLK
    meta = jnp.broadcast_to(meta_tokens.astype(x.dtype)[None], (bsz, N_META, dm))
    h = jnp.concatenate([jnp.zeros((bsz, PAD, dm), x.dtype), meta, x], axis=1).reshape(bsz * lp, dm)
    final_w = final_norm_w.astype(F32).reshape(1, dm)
    for l in range(depth):
        w_main, w_small = _permute_w_in(w_in[l])
        u, small, small_t = _inproj(h, norm1_w[l].astype(F32).reshape(1, dm), w_main, w_small)
        y_gdn = _gdn(u, small, small_t, bsz, nb, gdn_conv_w[l], gdn_a_log[l], gdn_dt_bias[l], gdn_norm_w[l])
        y_ssd = _ssd(u, small, small_t, bsz, nb, ssd_conv_w[l], ssd_conv_b[l], ssd_dt_bias[l], ssd_a_log[l],
                     ssd_d[l], ssd_norm_w[l])
        y_swa = _swa(u, bsz, nb, swa_sinks[l])
        h = _merge(h, y_gdn, y_ssd, y_swa, u, w_proj_gdn[l].astype(BF16), w_proj_ssd[l].astype(BF16),
                   w_proj_swa[l].astype(BF16), w_out[l].astype(BF16))
        last = l == depth - 1
        h = _mlp(h, norm2_w[l].astype(F32).reshape(1, dm), w_up[l].astype(BF16), w_down[l].astype(BF16), final_w,
                 final_norm=last, seq_rows=(bsz, lp, seq) if last else None)
    return h.reshape(bsz, seq, dm)
```

```python
import functools

import jax
import jax.numpy as jnp
from jax import lax
from jax.experimental import pallas as pl
from jax.experimental.pallas import tpu as pltpu

F32 = jnp.float32
BF16 = jnp.bfloat16

D_MODEL = 1024
D_FF = 4 * D_MODEL
N_META = 16
RMS_EPS = 1e-6
L2_EPS = 1e-6

BLK = 128
PAD = BLK - N_META
HALO = 16

GDN_HEADS = 8
GDN_D = 128
GDN_GROUP = 2
SSD_HEADS = 16
SSD_P = 64
SSD_GROUPS = 4
SSD_N = 128
SSD_GROUP = 1
SWA_Q_HEADS = 16
SWA_KV_HEADS = 4
SWA_DH = 64
SINK_PAD = 8

COL_GDN_Q, COL_GDN_K, COL_GDN_V, COL_GDN_GATE = 0, 1024, 2048, 3072
COL_SSD_Z, COL_SSD_X, COL_SSD_BC = 4096, 5120, 6144
COL_SWA_Q = 7168
COL_GATES = 8192
COL_SWA_K, COL_SWA_V = 11264, 11520
U_WIDTH = 11776
SMALL_W = 128
SMALL_ROWS = 32
U_DTYPE = BF16

VMEM_LIMIT = 52 * 1024 * 1024


def _cparams(sem):
    return pltpu.CompilerParams(dimension_semantics=sem, vmem_limit_bytes=VMEM_LIMIT)


def _pick_tile(n, candidates):
    for c in candidates:
        if n % c == 0:
            return c
    raise ValueError(f"no tile for {n}")


def _dot(a, b):
    return jnp.dot(a.astype(BF16), b.astype(BF16), preferred_element_type=F32)


def _dot_nt(a, b):
    return lax.dot_general(a.astype(BF16), b.astype(BF16), (((1,), (1,)), ((), ())), preferred_element_type=F32)


def _dot_tn(a, b):
    return lax.dot_general(a.astype(BF16), b.astype(BF16), (((0,), (0,)), ((), ())), preferred_element_type=F32)


def _split3_bf16(a):
    p1 = a.astype(BF16)
    r1 = a - p1.astype(F32)
    p2 = r1.astype(BF16)
    p3 = (r1 - p2.astype(F32)).astype(BF16)
    return p1, p2, p3


def _dot_sel_rhs(a, sel):
    s = sel.astype(F32).astype(BF16)
    return sum(jnp.dot(p, s, preferred_element_type=F32) for p in _split3_bf16(a))


def _dot_sel_lhs(sel, b):
    s = sel.astype(F32).astype(BF16)
    return sum(jnp.dot(s, p, preferred_element_type=F32) for p in _split3_bf16(b))


def _sigmoid(x):
    return 0.5 * jnp.tanh(0.5 * x) + 0.5


def _silu(x):
    h = 0.5 * x
    return h * jnp.tanh(h) + h


def _softplus(x):
    return jnp.maximum(x, 0.0) + jnp.log1p(jnp.exp(-jnp.abs(x)))


def _rmsnorm(x, w):
    return x * lax.rsqrt(jnp.mean(x * x, axis=-1, keepdims=True) + RMS_EPS) * w


def _shift_select(dtype):
    r = jnp.arange(3 * BLK)[:, None]
    m = jnp.arange(HALO + BLK)[None, :]
    return (m == HALO + r % BLK - 3 + r // BLK).astype(dtype)


def _causal_conv4(x, halo, w, sel):
    c = x.shape[0]
    z = jnp.dot(sel, jnp.concatenate([halo, x], axis=0), preferred_element_type=F32)
    return x.astype(F32) * w[3:4] + z[0:c] * w[0:1] + z[c:2 * c] * w[1:2] + z[2 * c:] * w[2:3]


SOLVE_BASE = 16


def _solve_unit_lower(a_list, x_list):
    c = a_list[0].shape[0]
    ri = lax.broadcasted_iota(jnp.int32, (c, c), 0)
    ci = lax.broadcasted_iota(jnp.int32, (c, c), 1)

    def blk(x, size):
        return lax.shift_right_logical(x, size.bit_length() - 1)

    diag = blk(ri, SOLVE_BASE) == blk(ci, SOLVE_BASE)
    ps = [jnp.where(diag, a, 0.0) for a in a_list]
    ts = [jnp.where(ri == ci, 1.0, 0.0) - p for p in ps]
    size = 2
    while size < SOLVE_BASE:
        ps = [_dot(p, p) for p in ps]
        ts = [t + _dot(p, t) for p, t in zip(ps, ts)]
        size *= 2
    size = SOLVE_BASE
    while size < c:
        off = (blk(ri, 2 * size) == blk(ci, 2 * size)) & (blk(ri, size) != blk(ci, size))
        ots = [_dot(jnp.where(off, a, 0.0), t) for a, t in zip(a_list, ts)]
        ts = [t - _dot(t, ot) for t, ot in zip(ts, ots)]
        size *= 2
    return [_dot(t, x) for t, x in zip(ts, x_list)]


def _inproj_kernel(h_ref, nw_ref, w_ref, ws_ref, u_ref, small_ref, smallt_ref, hn_ref):
    @pl.when(pl.program_id(1) == 0)
    def _():
        hn = _rmsnorm(h_ref[...], nw_ref[...]).astype(BF16)
        hn_ref[...] = hn
        small = jnp.dot(hn, ws_ref[...], preferred_element_type=F32)
        small_ref[...] = small
        smallt_ref[...] = small.T[:SMALL_ROWS]

    u_ref[...] = jnp.dot(hn_ref[...], w_ref[...], preferred_element_type=F32).astype(u_ref.dtype)


def _inproj(h, norm_w, w_main, w_small):
    t = h.shape[0]
    tm = _pick_tile(t, (3072, 1536, 1024, 768, 512, 384, 256, 128))
    tn = 512
    return pl.pallas_call(
        _inproj_kernel,
        grid=(t // tm, U_WIDTH // tn),
        in_specs=[
            pl.BlockSpec((tm, D_MODEL), lambda i, j: (i, 0)),
            pl.BlockSpec((1, D_MODEL), lambda i, j: (0, 0)),
            pl.BlockSpec((D_MODEL, tn), lambda i, j: (0, j)),
            pl.BlockSpec((D_MODEL, SMALL_W), lambda i, j: (0, 0)),
        ],
        out_specs=[
            pl.BlockSpec((tm, tn), lambda i, j: (i, j)),
            pl.BlockSpec((tm, SMALL_W), lambda i, j: (i, 0)),
            pl.BlockSpec((SMALL_ROWS, tm), lambda i, j: (0, i)),
        ],
        out_shape=[
            jax.ShapeDtypeStruct((t, U_WIDTH), U_DTYPE),
            jax.ShapeDtypeStruct((t, SMALL_W), F32),
            jax.ShapeDtypeStruct((SMALL_ROWS, t), F32),
        ],
        scratch_shapes=[pltpu.VMEM((tm, D_MODEL), BF16)],
        compiler_params=_cparams(("parallel", "arbitrary")),
        name="inproj",
    )(h, norm_w, w_main, w_small)


def _gdn_kernel(q_ref, k_ref, v_ref, gate_ref, sm_ref, *rest):
    smt_refs = rest[:GDN_GROUP]
    sel_ref, cw_ref, alog_l_ref, dtb_l_ref, alog_c_ref, dtb_c_ref, nw_ref, o_ref, halo_ref, s_ref = rest[GDN_GROUP:]
    c = pl.program_id(1)

    @pl.when(c == 0)
    def _():
        halo_ref[...] = jnp.zeros_like(halo_ref)
        s_ref[...] = jnp.zeros_like(s_ref)

    n = GDN_HEADS * GDN_D
    cw = cw_ref[...]
    sel = sel_ref[...]
    ri = lax.broadcasted_iota(jnp.int32, (BLK, BLK), 0)
    ci = lax.broadcasted_iota(jnp.int32, (BLK, BLK), 1)
    incl = ri >= ci
    strict = ri > ci
    valid_rows = (c * BLK + ri) >= PAD
    ci8 = lax.broadcasted_iota(jnp.int32, (GDN_HEADS, BLK), 1)
    nw = nw_ref[...]
    seqs = range(GDN_GROUP)

    raw = [(q_ref[g], k_ref[g], v_ref[g]) for g in seqs]
    halos = [halo_ref[g] for g in seqs]
    conv = [[_silu(_causal_conv4(raw[g][j], halos[g][:, j * n:(j + 1) * n], cw[:, j * n:(j + 1) * n], sel))
             for j in range(3)] for g in seqs]
    for g in seqs:
        for j in range(3):
            halo_ref[g, :, j * n:(j + 1) * n] = raw[g][j][BLK - HALO:BLK]

    sms = [sm_ref[g] for g in seqs]
    beta_all = [jnp.where(valid_rows, _sigmoid(sm), 0.0) for sm in sms]
    g_all = [jnp.where(valid_rows, -jnp.exp(alog_l_ref[...]) * _softplus(sm + dtb_l_ref[...]), 0.0) for sm in sms]
    gam_all = [_dot_sel_lhs(incl, x) for x in g_all]
    g_rows = [jnp.where((c * BLK + ci8) >= PAD,
                        -jnp.exp(alog_c_ref[...]) * _softplus(smt_refs[g][8:16, :] + dtb_c_ref[...]), 0.0)
              for g in seqs]
    gam_rows = [_dot_sel_rhs(x, ri <= ci) for x in g_rows]

    items = [(g, h) for g in seqs for h in range(GDN_HEADS)]
    idx = range(len(items))
    sls = [slice(h * GDN_D, (h + 1) * GDN_D) for _, h in items]
    qs = [conv[g][0][:, sls[i]] for i, (g, h) in enumerate(items)]
    ks = [conv[g][1][:, sls[i]] for i, (g, h) in enumerate(items)]
    vs = [conv[g][2][:, sls[i]] for i, (g, h) in enumerate(items)]
    qs = [q * lax.rsqrt(jnp.sum(q * q, axis=-1, keepdims=True) + L2_EPS) * (GDN_D ** -0.5) for q in qs]
    ks = [k * lax.rsqrt(jnp.sum(k * k, axis=-1, keepdims=True) + L2_EPS) for k in ks]
    betas = [beta_all[g][:, h:h + 1] for g, h in items]
    gam_cs = [gam_all[g][:, 8 + h:9 + h] for g, h in items]
    dmats = [jnp.exp(jnp.where(incl, gam_cs[i] - gam_rows[g][h:h + 1, :], -jnp.inf)) for i, (g, h) in enumerate(items)]
    kbs = [ks[i] * betas[i] for i in idx]
    egams = [jnp.exp(x) for x in gam_cs]
    a_list = [jnp.where(strict, _dot_nt(kbs[i], ks[i]) * dmats[i], 0.0) for i in idx]
    rhs = [jnp.concatenate([vs[i] * betas[i], kbs[i] * egams[i]], axis=1) for i in idx]
    sols = _solve_unit_lower(a_list, rhs)
    aqks = [_dot_nt(qs[i], ks[i]) * dmats[i] for i in idx]
    glasts = [x[BLK - 1:BLK, :] for x in gam_cs]
    ktails = [ks[i] * jnp.exp(glasts[i] - gam_cs[i]) for i in idx]
    states = [s_ref[i] for i in idx]
    wss = [_dot(jnp.concatenate([sols[i][:, GDN_D:], qs[i] * egams[i]], axis=0), states[i]) for i in idx]
    vnews = [sols[i][:, :GDN_D] - wss[i][:BLK] for i in idx]
    outs = [wss[i][BLK:] + _dot(aqks[i], vnews[i]) for i in idx]
    for i in idx:
        s_ref[i] = states[i] * jnp.exp(glasts[i]) + _dot_tn(ktails[i], vnews[i])
    for i, (g, h) in enumerate(items):
        gt = gate_ref[g, :, sls[i]].astype(F32)
        o_ref[g, :, sls[i]] = (_rmsnorm(outs[i], nw) * _silu(gt)).astype(o_ref.dtype)


def _gdn(u, small, small_t, bsz, nb, conv_w, a_log, dt_bias, norm_w):
    t = u.shape[0]
    n = GDN_HEADS * GDN_D
    grp = GDN_GROUP
    assert bsz % grp == 0
    u3 = u.reshape(bsz, nb * BLK, U_WIDTH)
    small3 = small.reshape(bsz, nb * BLK, SMALL_W)

    def lane_vec(v, off):
        return jnp.zeros((1, SMALL_W), F32).at[0, off:off + v.shape[0]].set(v.astype(F32))

    def ublk(col):
        return pl.BlockSpec((grp, BLK, n), lambda b, c: (b, c, col // n))

    def full(shape):
        return pl.BlockSpec(shape, lambda b, c: (0,) * len(shape))

    def smallt_spec(g):
        return pl.BlockSpec((SMALL_ROWS, BLK), lambda b, c: (0, (b * grp + g) * nb + c))

    y = pl.pallas_call(
        _gdn_kernel,
        grid=(bsz // grp, nb),
        in_specs=[
            ublk(COL_GDN_Q), ublk(COL_GDN_K), ublk(COL_GDN_V), ublk(COL_GDN_GATE),
            pl.BlockSpec((grp, BLK, SMALL_W), lambda b, c: (b, c, 0)),
            *[smallt_spec(g) for g in range(grp)],
            full((3 * BLK, HALO + BLK)), full((4, 3 * n)), full((1, SMALL_W)), full((1, SMALL_W)),
            full((GDN_HEADS, 1)), full((GDN_HEADS, 1)), full((1, GDN_D)),
        ],
        out_specs=pl.BlockSpec((grp, BLK, n), lambda b, c: (b, c, 0)),
        out_shape=jax.ShapeDtypeStruct((bsz, nb * BLK, n), BF16),
        scratch_shapes=[pltpu.VMEM((grp, HALO, 3 * n), U_DTYPE), pltpu.VMEM((grp * GDN_HEADS, GDN_D, GDN_D), F32)],
        compiler_params=_cparams(("parallel", "arbitrary")),
        name="gdn_mixer",
    )(u3, u3, u3, u3, small3, *([small_t] * grp), _shift_select(U_DTYPE), conv_w.astype(F32), lane_vec(a_log, 8),
      lane_vec(dt_bias, 8), a_log.astype(F32).reshape(GDN_HEADS, 1), dt_bias.astype(F32).reshape(GDN_HEADS, 1),
      norm_w.astype(F32).reshape(1, GDN_D))
    return y.reshape(t, n)


def _ssd_kernel(z_ref, x_ref, bc_ref, sm_ref, *rest):
    smt_refs = rest[:SSD_GROUP]
    (sel_ref, cw_ref, cb_ref, alog_l_ref, dtb_l_ref, alog_c_ref, dtb_c_ref, d_ref, nw_ref, e_ref, o_ref, halo_ref,
     st_ref) = rest[SSD_GROUP:]
    c = pl.program_id(1)

    @pl.when(c == 0)
    def _():
        halo_ref[...] = jnp.zeros_like(halo_ref)
        st_ref[...] = jnp.zeros_like(st_ref)

    n = SSD_HEADS * SSD_P
    gn = SSD_GROUPS * SSD_N
    gw = n // SSD_GROUPS
    hpg = SSD_HEADS // SSD_GROUPS
    cw = cw_ref[...]
    cb = cb_ref[...]
    sel = sel_ref[...]
    e = e_ref[...]
    ri = lax.broadcasted_iota(jnp.int32, (BLK, BLK), 0)
    ci = lax.broadcasted_iota(jnp.int32, (BLK, BLK), 1)
    incl = ri >= ci
    valid_rows = (c * BLK + ri) >= PAD
    ci16 = lax.broadcasted_iota(jnp.int32, (SSD_HEADS, BLK), 1)
    valid_out = (c * BLK + lax.broadcasted_iota(jnp.int32, (BLK, gw), 0)) >= PAD
    seqs = range(SSD_GROUP)

    raw = [(x_ref[s], bc_ref[s]) for s in seqs]
    halos = [halo_ref[s] for s in seqs]
    xs = [_silu(_causal_conv4(raw[s][0], halos[s][:, 0:n], cw[:, 0:n], sel) + cb[:, 0:n]) for s in seqs]
    bcs = [_silu(_causal_conv4(raw[s][1], halos[s][:, n:2 * n], cw[:, n:2 * n], sel) + cb[:, n:2 * n]) for s in seqs]
    for s in seqs:
        halo_ref[s, :, 0:n] = raw[s][0][BLK - HALO:BLK]
        halo_ref[s, :, n:2 * n] = raw[s][1][BLK - HALO:BLK]

    dtp_all = [jnp.where(valid_rows, _softplus(sm_ref[s] + dtb_l_ref[...]), 0.0) for s in seqs]
    acum_all = [_dot_sel_lhs(incl, x * (-jnp.exp(alog_l_ref[...]))) for x in dtp_all]
    dtp_x = [_dot_sel_rhs(x, e) for x in dtp_all]
    acum_x = [_dot_sel_rhs(x, e) for x in acum_all]
    dtp_rows = [jnp.where((c * BLK + ci16) >= PAD, _softplus(smt_refs[s][16:32, :] + dtb_c_ref[...]), 0.0) for s in seqs]
    acum_rows = [_dot_sel_rhs(x * (-jnp.exp(alog_c_ref[...])), ri <= ci) for x in dtp_rows]

    xdt = [xs[s] * dtp_x[s] for s in seqs]
    eacum_x = [jnp.exp(x) for x in acum_x]
    alast_x = [x[BLK - 1:BLK, :] for x in acum_x]
    xdt_dec = [xdt[s] * jnp.exp(alast_x[s] - acum_x[s]) for s in seqs]

    sg = [(s, g) for s in seqs for g in range(SSD_GROUPS)]
    sh = [(s, hh) for s in seqs for hh in range(SSD_HEADS)]
    gsl = [slice(g * gw, (g + 1) * gw) for g in range(SSD_GROUPS)]
    bgs = {(s, g): bcs[s][:, g * SSD_N:(g + 1) * SSD_N] for s, g in sg}
    cgs = {(s, g): bcs[s][:, gn + g * SSD_N:gn + (g + 1) * SSD_N] for s, g in sg}
    cbms = {k: _dot_nt(cgs[k], bgs[k]) for k in sg}
    sts = {(s, g): st_ref[s, :, gsl[g]] for s, g in sg}
    yoffs = {(s, g): _dot(cgs[s, g], sts[s, g]) * eacum_x[s][:, gsl[g]] for s, g in sg}
    lms = {(s, hh): jnp.exp(jnp.where(incl, acum_all[s][:, 16 + hh:17 + hh] - acum_rows[s][hh:hh + 1, :], -jnp.inf))
           for s, hh in sh}
    ydiag = {(s, hh): _dot(cbms[s, hh // hpg] * lms[s, hh], xdt[s][:, hh * SSD_P:(hh + 1) * SSD_P]) for s, hh in sh}
    for s, g in sg:
        st_ref[s, :, gsl[g]] = (sts[s, g] * jnp.exp(alast_x[s][:, gsl[g]])
                                + _dot_tn(bgs[s, g], xdt_dec[s][:, gsl[g]]))
    for s, g in sg:
        gs = gsl[g]
        y = yoffs[s, g] + jnp.concatenate([ydiag[s, hh] for hh in range(g * hpg, (g + 1) * hpg)], axis=1)
        y = y + d_ref[:, gs] * xs[s][:, gs]
        y = y * _silu(z_ref[s, :, gs].astype(F32))
        y = _rmsnorm(y, nw_ref[:, gs])
        o_ref[s, :, gs] = jnp.where(valid_out, y, 0.0).astype(o_ref.dtype)


def _ssd(u, small, small_t, bsz, nb, conv_w, conv_b, dt_bias, a_log, d_skip, norm_w):
    t = u.shape[0]
    n = SSD_HEADS * SSD_P
    grp = SSD_GROUP
    assert bsz % grp == 0
    u3 = u.reshape(bsz, nb * BLK, U_WIDTH)
    small3 = small.reshape(bsz, nb * BLK, SMALL_W)

    def lane_vec(v, off):
        return jnp.zeros((1, SMALL_W), F32).at[0, off:off + v.shape[0]].set(v.astype(F32))

    def ublk(col):
        return pl.BlockSpec((grp, BLK, n), lambda b, c: (b, c, col // n))

    def full(shape):
        return pl.BlockSpec(shape, lambda b, c: (0,) * len(shape))

    def smallt_spec(g):
        return pl.BlockSpec((SMALL_ROWS, BLK), lambda b, c: (0, (b * grp + g) * nb + c))

    expand = jnp.zeros((SMALL_W, n), F32).at[16:16 + SSD_HEADS].set(
        jnp.repeat(jnp.eye(SSD_HEADS, dtype=F32), SSD_P, axis=1))
    y = pl.pallas_call(
        _ssd_kernel,
        grid=(bsz // grp, nb),
        in_specs=[
            ublk(COL_SSD_Z), ublk(COL_SSD_X), ublk(COL_SSD_BC),
            pl.BlockSpec((grp, BLK, SMALL_W), lambda b, c: (b, c, 0)),
            *[smallt_spec(g) for g in range(grp)],
            full((3 * BLK, HALO + BLK)), full((4, 2 * n)), full((1, 2 * n)), full((1, SMALL_W)), full((1, SMALL_W)),
            full((SSD_HEADS, 1)), full((SSD_HEADS, 1)), full((1, n)), full((1, n)), full((SMALL_W, n)),
        ],
        out_specs=pl.BlockSpec((grp, BLK, n), lambda b, c: (b, c, 0)),
        out_shape=jax.ShapeDtypeStruct((bsz, nb * BLK, n), BF16),
        scratch_shapes=[pltpu.VMEM((grp, HALO, 2 * n), U_DTYPE), pltpu.VMEM((grp, SSD_N, n), F32)],
        compiler_params=_cparams(("parallel", "arbitrary")),
        name="ssd_mixer",
    )(u3, u3, u3, small3, *([small_t] * grp), _shift_select(U_DTYPE), conv_w.astype(F32),
      conv_b.astype(F32).reshape(1, 2 * n), lane_vec(a_log, 16), lane_vec(dt_bias, 16),
      a_log.astype(F32).reshape(SSD_HEADS, 1), dt_bias.astype(F32).reshape(SSD_HEADS, 1),
      jnp.repeat(d_skip.astype(F32), SSD_P).reshape(1, n), norm_w.astype(F32).reshape(1, n), expand)
    return y.reshape(t, n)


def _swa_kernel(sink_ref, q_ref, kc_ref, kp_ref, km_ref, vc_ref, vp_ref, vm_ref, o_ref):
    c = pl.program_id(1)
    ri = lax.broadcasted_iota(jnp.int32, (BLK, BLK), 0)
    ci = lax.broadcasted_iota(jnp.int32, (BLK, BLK), 1)
    cur_ok = (ci <= ri) & ((c > 0) | (ci >= PAD))
    prev_ok = ((c >= 2) & (ci > ri)) | ((c == 1) & (ci >= PAD))
    rep = SWA_Q_HEADS // SWA_KV_HEADS
    neg = -jnp.inf
    bias_pc = jnp.concatenate([jnp.where(prev_ok, 0.0, neg), jnp.where(cur_ok, 0.0, neg)], axis=1)
    bias_pc = jnp.concatenate([bias_pc] * rep, axis=0)
    lo = ci < SWA_DH
    lo_m = lax.broadcasted_iota(jnp.int32, (N_META, BLK), 1) < SWA_DH
    row4 = lax.broadcasted_iota(jnp.int32, (rep * BLK, 1), 0)

    def both_halves(ref, hk, lo_mask):
        x = ref[:, (hk // 2) * BLK:(hk // 2 + 1) * BLK].astype(F32)
        x = jnp.where(lo_mask, x, 0.0) if hk % 2 == 0 else jnp.where(lo_mask, 0.0, x)
        return x + pltpu.roll(x, SWA_DH, axis=1)

    def stacked_queries(hk):
        scale = SWA_DH ** -0.5
        qa = q_ref[:, (2 * hk) * BLK:(2 * hk + 1) * BLK].astype(F32) * scale
        qb = q_ref[:, (2 * hk + 1) * BLK:(2 * hk + 2) * BLK].astype(F32) * scale
        return jnp.concatenate([jnp.where(lo, qa, 0.0), jnp.where(lo, 0.0, qa),
                                jnp.where(lo, qb, 0.0), jnp.where(lo, 0.0, qb)], axis=0)

    kvh = range(SWA_KV_HEADS)
    qs = [stacked_queries(hk) for hk in kvh]
    k_pc = [jnp.concatenate([both_halves(kp_ref, hk, lo), both_halves(kc_ref, hk, lo)], axis=0) for hk in kvh]
    zrows = jnp.zeros((SINK_PAD, BLK), F32)
    k_m = [jnp.concatenate([both_halves(km_ref, hk, lo_m), zrows], axis=0) for hk in kvh]
    v_pc = [jnp.concatenate([both_halves(vp_ref, hk, lo), both_halves(vc_ref, hk, lo)], axis=0) for hk in kvh]
    v_m = [jnp.concatenate([both_halves(vm_ref, hk, lo_m), zrows], axis=0) for hk in kvh]
    col_m = lax.broadcasted_iota(jnp.int32, (1, N_META + SINK_PAD), 1)
    base_m = jnp.where((col_m < N_META) & (c >= 2), 0.0, neg)

    def sink_column(hk):
        col = jnp.full((rep * BLK, 1), sink_ref[hk * rep + rep - 1], F32)
        for r in range(rep - 2, -1, -1):
            col = jnp.where(row4 < (r + 1) * BLK, sink_ref[hk * rep + r], col)
        return col

    bias_m = [jnp.where(col_m == N_META, sink_column(hk), base_m) for hk in kvh]
    s_pc = [_dot_nt(qs[hk], k_pc[hk]) + bias_pc for hk in kvh]
    s_m = [_dot_nt(qs[hk], k_m[hk]) + bias_m[hk] for hk in kvh]
    ms = [jnp.maximum(jnp.max(s_pc[hk], axis=-1, keepdims=True), jnp.max(s_m[hk], axis=-1, keepdims=True))
          for hk in kvh]
    e_pc = [jnp.exp(s_pc[hk] - ms[hk]) for hk in kvh]
    e_m = [jnp.exp(s_m[hk] - ms[hk]) for hk in kvh]
    invs = [1.0 / (jnp.sum(e_pc[hk], axis=-1, keepdims=True) + jnp.sum(e_m[hk], axis=-1, keepdims=True)) for hk in kvh]
    outs = [(_dot(e_pc[hk], v_pc[hk]) + _dot(e_m[hk], v_m[hk])) * invs[hk] for hk in kvh]
    for hk in kvh:
        o = outs[hk]
        o_ref[:, (2 * hk) * BLK:(2 * hk + 1) * BLK] = jnp.where(lo, o[0:BLK], o[BLK:2 * BLK]).astype(o_ref.dtype)
        o_ref[:, (2 * hk + 1) * BLK:(2 * hk + 2) * BLK] = jnp.where(lo, o[2 * BLK:3 * BLK],
                                                                     o[3 * BLK:]).astype(o_ref.dtype)


def _swa(u, bsz, nb, sinks):
    t = u.shape[0]
    nq = SWA_Q_HEADS * SWA_DH
    nkv = SWA_KV_HEADS * SWA_DH
    meta_blk = PAD // N_META
    rows16 = (nb * BLK) // N_META

    def cur(col):
        return pl.BlockSpec((BLK, nkv), lambda b, c: (b * nb + c, col // nkv))

    def prev(col):
        return pl.BlockSpec((BLK, nkv), lambda b, c: (b * nb + jnp.maximum(c - 1, 0), col // nkv))

    def meta(col):
        return pl.BlockSpec((N_META, nkv), lambda b, c: (b * rows16 + meta_blk, col // nkv))

    return pl.pallas_call(
        _swa_kernel,
        grid=(bsz, nb),
        in_specs=[
            pl.BlockSpec(memory_space=pltpu.SMEM),
            pl.BlockSpec((BLK, nq), lambda b, c: (b * nb + c, COL_SWA_Q // nq)),
            cur(COL_SWA_K), prev(COL_SWA_K), meta(COL_SWA_K),
            cur(COL_SWA_V), prev(COL_SWA_V), meta(COL_SWA_V),
        ],
        out_specs=pl.BlockSpec((BLK, nq), lambda b, c: (b * nb + c, 0)),
        out_shape=jax.ShapeDtypeStruct((t, nq), BF16),
        compiler_params=_cparams(("parallel", "parallel")),
        name="swa_mixer",
    )(sinks.astype(F32), u, u, u, u, u, u, u)


def _merge_kernel(h_ref, yg_ref, ys_ref, yc_ref, ga_ref, gb_ref, gc_ref, wg_ref, ws_ref, wc_ref, wo_ref, o_ref):
    d = functools.partial(jnp.dot, preferred_element_type=F32)
    merged = (_sigmoid(ga_ref[...].astype(F32)) * d(yg_ref[...], wg_ref[...])
              + _sigmoid(gb_ref[...].astype(F32)) * d(ys_ref[...], ws_ref[...])
              + _sigmoid(gc_ref[...].astype(F32)) * d(yc_ref[...], wc_ref[...]))
    o_ref[...] = h_ref[...] + d(merged.astype(BF16), wo_ref[...])


def _merge(h, y_gdn, y_ssd, y_swa, u, w_gdn, w_ssd, w_swa, w_out):
    t = h.shape[0]
    tm = _pick_tile(t, (512, 384, 256, 128))
    row = pl.BlockSpec((tm, D_MODEL), lambda i: (i, 0))
    wfull = pl.BlockSpec((D_MODEL, D_MODEL), lambda i: (0, 0))

    def gate(k):
        return pl.BlockSpec((tm, D_MODEL), lambda i: (i, COL_GATES // D_MODEL + k))

    return pl.pallas_call(
        _merge_kernel,
        grid=(t // tm,),
        in_specs=[row, row, row, row, gate(0), gate(1), gate(2), wfull, wfull, wfull, wfull],
        out_specs=row,
        out_shape=jax.ShapeDtypeStruct((t, D_MODEL), F32),
        compiler_params=_cparams(("parallel",)),
        name="merge",
    )(h, y_gdn, y_ssd, y_swa, u, u, u, w_gdn, w_ssd, w_swa, w_out)


def _mlp_kernel(h_ref, nw_ref, wu_ref, wd_ref, fw_ref, o_ref, hn_ref, acc_ref, *, final_norm):
    j = pl.program_id(1)

    @pl.when(j == 0)
    def _():
        hn_ref[...] = _rmsnorm(h_ref[...], nw_ref[...]).astype(BF16)
        acc_ref[...] = jnp.zeros_like(acc_ref)

    up = jnp.maximum(jnp.dot(hn_ref[...], wu_ref[...], preferred_element_type=F32), 0.0)
    acc_ref[...] += jnp.dot((up * up).astype(BF16), wd_ref[...], preferred_element_type=F32)

    @pl.when(j == pl.num_programs(1) - 1)
    def _():
        r = h_ref[...] + acc_ref[...]
        if final_norm:
            r = _rmsnorm(r, fw_ref[...])
        o_ref[...] = r


def _mlp(h, norm_w, w_up, w_down, final_w, final_norm, seq_rows=None):
    t = h.shape[0]
    tf = 1024
    if seq_rows is None:
        tm = _pick_tile(t, (1024, 768, 512, 384, 256, 128))
        n_rows = t
        h_spec = pl.BlockSpec((tm, D_MODEL), lambda i, j: (i, 0))
    else:
        bsz, lp, seq = seq_rows
        tm = _pick_tile(seq, (1024, 768, 512, 384, 256, 128))
        per_seq = seq // tm
        n_rows = bsz * seq
        h_spec = pl.BlockSpec((pl.Element(tm), pl.Element(D_MODEL)),
                              lambda i, j: (pl.multiple_of((i // per_seq) * lp + (lp - seq) + (i % per_seq) * tm, BLK), 0))
    return pl.pallas_call(
        functools.partial(_mlp_kernel, final_norm=final_norm),
        grid=(n_rows // tm, D_FF // tf),
        in_specs=[
            h_spec,
            pl.BlockSpec((1, D_MODEL), lambda i, j: (0, 0)),
            pl.BlockSpec((D_MODEL, tf), lambda i, j: (0, j)),
            pl.BlockSpec((tf, D_MODEL), lambda i, j: (j, 0)),
            pl.BlockSpec((1, D_MODEL), lambda i, j: (0, 0)),
        ],
        out_specs=pl.BlockSpec((tm, D_MODEL), lambda i, j: (i, 0)),
        out_shape=jax.ShapeDtypeStruct((n_rows, D_MODEL), F32),
        scratch_shapes=[pltpu.VMEM((tm, D_MODEL), BF16), pltpu.VMEM((tm, D_MODEL), F32)],
        compiler_params=_cparams(("parallel", "arbitrary")),
        name="mlp",
    )(h, norm_w, w_up, w_down, final_w)


def _permute_w_in(w):
    a_small = 4 * GDN_HEADS * GDN_D
    b_z = a_small + 2 * GDN_HEADS
    b_dt = b_z + 3 * SSD_HEADS * SSD_P
    c_q = b_dt + SSD_HEADS
    c_k = c_q + SWA_Q_HEADS * SWA_DH
    c_v = c_k + SWA_KV_HEADS * SWA_DH
    gates = c_v + SWA_KV_HEADS * SWA_DH
    main = jnp.concatenate([w[:, :a_small], w[:, b_z:b_dt], w[:, c_q:c_k], w[:, gates:], w[:, c_k:c_v],
                            w[:, c_v:gates]], axis=1)
    small = jnp.concatenate([w[:, a_small:b_z], w[:, b_dt:c_q],
                             jnp.zeros((w.shape[0], SMALL_W - SMALL_ROWS), w.dtype)], axis=1)
    return main.astype(BF16), small.astype(BF16)


def kernel(x, meta_tokens, norm1_w, w_in, gdn_conv_w, gdn_a_log, gdn_dt_bias, gdn_norm_w, ssd_conv_w, ssd_conv_b,
           ssd_dt_bias, ssd_a_log, ssd_d, ssd_norm_w, swa_sinks, w_proj_gdn, w_proj_ssd, w_proj_swa, w_out, norm2_w,
           w_up, w_down, final_norm_w):
    bsz, seq, dm = x.shape
    depth = w_in.shape[0]
    assert dm == D_MODEL and seq % BLK == 0
    nb = seq // BLK + 1
    lp = nb * BLK
    meta = jnp.broadcast_to(meta_tokens.astype(x.dtype)[None], (bsz, N_META, dm))
    h = jnp.concatenate([jnp.zeros((bsz, PAD, dm), x.dtype), meta, x], axis=1).reshape(bsz * lp, dm)
    final_w = final_norm_w.astype(F32).reshape(1, dm)
    for l in range(depth):
        w_main, w_small = _permute_w_in(w_in[l])
        u, small, small_t = _inproj(h, norm1_w[l].astype(F32).reshape(1, dm), w_main, w_small)
        y_gdn = _gdn(u, small, small_t, bsz, nb, gdn_conv_w[l], gdn_a_log[l], gdn_dt_bias[l], gdn_norm_w[l])
        y_ssd = _ssd(u, small, small_t, bsz, nb, ssd_conv_w[l], ssd_conv_b[l], ssd_dt_bias[l], ssd_a_log[l],
                     ssd_d[l], ssd_norm_w[l])
        y_swa = _swa(u, bsz, nb, swa_sinks[l])
        h = _merge(h, y_gdn, y_ssd, y_swa, u, w_proj_gdn[l].astype(BF16), w_proj_ssd[l].astype(BF16),
                   w_proj_swa[l].astype(BF16), w_out[l].astype(BF16))
        last = l == depth - 1
        h = _mlp(h, norm2_w[l].astype(F32).reshape(1, dm), w_up[l].astype(BF16), w_down[l].astype(BF16), final_w,
                 final_norm=last, seq_rows=(bsz, lp, seq) if last else None)
    return h.reshape(bsz, seq, dm)
```

```python
import functools

import jax
import jax.numpy as jnp
from jax import lax
from jax.experimental import pallas as pl
from jax.experimental.pallas import tpu as pltpu

F32 = jnp.float32
BF16 = jnp.bfloat16

D_MODEL = 1024
D_FF = 4 * D_MODEL
N_META = 16
RMS_EPS = 1e-6
L2_EPS = 1e-6

BLK = 128
PAD = BLK - N_META
HALO = 16

GDN_HEADS = 8
GDN_D = 128
GDN_GROUP = 2
SSD_HEADS = 16
SSD_P = 64
SSD_GROUPS = 4
SSD_N = 128
SSD_GROUP = 1
SWA_Q_HEADS = 16
SWA_KV_HEADS = 4
SWA_DH = 64
SINK_PAD = 8

COL_GDN_Q, COL_GDN_K, COL_GDN_V, COL_GDN_GATE = 0, 1024, 2048, 3072
COL_SSD_Z, COL_SSD_X, COL_SSD_BC = 4096, 5120, 6144
COL_SWA_Q = 7168
COL_GATES = 8192
COL_SWA_K, COL_SWA_V = 11264, 11520
U_WIDTH = 11776
SMALL_W = 128
SMALL_ROWS = 32
U_DTYPE = BF16

VMEM_LIMIT = 52 * 1024 * 1024


def _cparams(sem):
    return pltpu.CompilerParams(dimension_semantics=sem, vmem_limit_bytes=VMEM_LIMIT)


def _pick_tile(n, candidates):
    for c in candidates:
        if n % c == 0:
            return c
    raise ValueError(f"no tile for {n}")


def _dot(a, b):
    return jnp.dot(a.astype(BF16), b.astype(BF16), preferred_element_type=F32)


def _dot_nt(a, b):
    return lax.dot_general(a.astype(BF16), b.astype(BF16), (((1,), (1,)), ((), ())), preferred_element_type=F32)


def _dot_tn(a, b):
    return lax.dot_general(a.astype(BF16), b.astype(BF16), (((0,), (0,)), ((), ())), preferred_element_type=F32)


def _split3_bf16(a):
    p1 = a.astype(BF16)
    r1 = a - p1.astype(F32)
    p2 = r1.astype(BF16)
    p3 = (r1 - p2.astype(F32)).astype(BF16)
    return p1, p2, p3


def _dot_sel_rhs(a, sel):
    s = sel.astype(F32).astype(BF16)
    return sum(jnp.dot(p, s, preferred_element_type=F32) for p in _split3_bf16(a))


def _dot_sel_lhs(sel, b):
    s = sel.astype(F32).astype(BF16)
    return sum(jnp.dot(s, p, preferred_element_type=F32) for p in _split3_bf16(b))


def _sigmoid(x):
    return 0.5 * jnp.tanh(0.5 * x) + 0.5


def _silu(x):
    h = 0.5 * x
    return h * jnp.tanh(h) + h


def _softplus(x):
    return jnp.maximum(x, 0.0) + jnp.log1p(jnp.exp(-jnp.abs(x)))


def _rmsnorm(x, w):
    return x * lax.rsqrt(jnp.mean(x * x, axis=-1, keepdims=True) + RMS_EPS) * w


def _shift_select(dtype):
    r = jnp.arange(3 * BLK)[:, None]
    m = jnp.arange(HALO + BLK)[None, :]
    return (m == HALO + r % BLK - 3 + r // BLK).astype(dtype)


def _causal_conv4(x, halo, w, sel):
    c = x.shape[0]
    z = jnp.dot(sel, jnp.concatenate([halo, x], axis=0), preferred_element_type=F32)
    return x.astype(F32) * w[3:4] + z[0:c] * w[0:1] + z[c:2 * c] * w[1:2] + z[2 * c:] * w[2:3]


SOLVE_BASE = 16


def _solve_unit_lower(a_list, x_list):
    c = a_list[0].shape[0]
    ri = lax.broadcasted_iota(jnp.int32, (c, c), 0)
    ci = lax.broadcasted_iota(jnp.int32, (c, c), 1)

    def blk(x, size):
        return lax.shift_right_logical(x, size.bit_length() - 1)

    diag = blk(ri, SOLVE_BASE) == blk(ci, SOLVE_BASE)
    ps = [jnp.where(diag, a, 0.0) for a in a_list]
    ts = [jnp.where(ri == ci, 1.0, 0.0) - p for p in ps]
    size = 2
    while size < SOLVE_BASE:
        ps = [_dot(p, p) for p in ps]
        ts = [t + _dot(p, t) for p, t in zip(ps, ts)]
        size *= 2
    size = SOLVE_BASE
    while size < c:
        off = (blk(ri, 2 * size) == blk(ci, 2 * size)) & (blk(ri, size) != blk(ci, size))
        ots = [_dot(jnp.where(off, a, 0.0), t) for a, t in zip(a_list, ts)]
        ts = [t - _dot(t, ot) for t, ot in zip(ts, ots)]
        size *= 2
    return [_dot(t, x) for t, x in zip(ts, x_list)]


def _inproj_kernel(h_ref, nw_ref, w_ref, ws_ref, u_ref, small_ref, smallt_ref, hn_ref):
    @pl.when(pl.program_id(1) == 0)
    def _():
        hn = _rmsnorm(h_ref[...], nw_ref[...]).astype(BF16)
        hn_ref[...] = hn
        small = jnp.dot(hn, ws_ref[...], preferred_element_type=F32)
        small_ref[...] = small
        smallt_ref[...] = small.T[:SMALL_ROWS]

    u_ref[...] = jnp.dot(hn_ref[...], w_ref[...], preferred_element_type=F32).astype(u_ref.dtype)


def _inproj(h, norm_w, w_main, w_small):
    t = h.shape[0]
    tm = _pick_tile(t, (3072, 1536, 1024, 768, 512, 384, 256, 128))
    tn = 512
    return pl.pallas_call(
        _inproj_kernel,
        grid=(t // tm, U_WIDTH // tn),
        in_specs=[
            pl.BlockSpec((tm, D_MODEL), lambda i, j: (i, 0)),
            pl.BlockSpec((1, D_MODEL), lambda i, j: (0, 0)),
            pl.BlockSpec((D_MODEL, tn), lambda i, j: (0, j)),
            pl.BlockSpec((D_MODEL, SMALL_W), lambda i, j: (0, 0)),
        ],
        out_specs=[
            pl.BlockSpec((tm, tn), lambda i, j: (i, j)),
            pl.BlockSpec((tm, SMALL_W), lambda i, j: (i, 0)),
            pl.BlockSpec((SMALL_ROWS, tm), lambda i, j: (0, i)),
        ],
        out_shape=[
            jax.ShapeDtypeStruct((t, U_WIDTH), U_DTYPE),
            jax.ShapeDtypeStruct((t, SMALL_W), F32),
            jax.ShapeDtypeStruct((SMALL_ROWS, t), F32),
        ],
        scratch_shapes=[pltpu.VMEM((tm, D_MODEL), BF16)],
        compiler_params=_cparams(("parallel", "arbitrary")),
        name="inproj",
    )(h, norm_w, w_main, w_small)


def _gdn_kernel(q_ref, k_ref, v_ref, gate_ref, sm_ref, *rest):
    smt_refs = rest[:GDN_GROUP]
    sel_ref, cw_ref, alog_l_ref, dtb_l_ref, alog_c_ref, dtb_c_ref, nw_ref, o_ref, halo_ref, s_ref = rest[GDN_GROUP:]
    c = pl.program_id(1)

    @pl.when(c == 0)
    def _():
        halo_ref[...] = jnp.zeros_like(halo_ref)
        s_ref[...] = jnp.zeros_like(s_ref)

    n = GDN_HEADS * GDN_D
    cw = cw_ref[...]
    sel = sel_ref[...]
    ri = lax.broadcasted_iota(jnp.int32, (BLK, BLK), 0)
    ci = lax.broadcasted_iota(jnp.int32, (BLK, BLK), 1)
    incl = ri >= ci
    strict = ri > ci
    valid_rows = (c * BLK + ri) >= PAD
    ci8 = lax.broadcasted_iota(jnp.int32, (GDN_HEADS, BLK), 1)
    nw = nw_ref[...]
    seqs = range(GDN_GROUP)

    raw = [(q_ref[g], k_ref[g], v_ref[g]) for g in seqs]
    halos = [halo_ref[g] for g in seqs]
    conv = [[_silu(_causal_conv4(raw[g][j], halos[g][:, j * n:(j + 1) * n], cw[:, j * n:(j + 1) * n], sel))
             for j in range(3)] for g in seqs]
    for g in seqs:
        for j in range(3):
            halo_ref[g, :, j * n:(j + 1) * n] = raw[g][j][BLK - HALO:BLK]

    sms = [sm_ref[g] for g in seqs]
    beta_all = [jnp.where(valid_rows, _sigmoid(sm), 0.0) for sm in sms]
    g_all = [jnp.where(valid_rows, -jnp.exp(alog_l_ref[...]) * _softplus(sm + dtb_l_ref[...]), 0.0) for sm in sms]
    gam_all = [_dot_sel_lhs(incl, x) for x in g_all]
    g_rows = [jnp.where((c * BLK + ci8) >= PAD,
                        -jnp.exp(alog_c_ref[...]) * _softplus(smt_refs[g][8:16, :] + dtb_c_ref[...]), 0.0)
              for g in seqs]
    gam_rows = [_dot_sel_rhs(x, ri <= ci) for x in g_rows]

    items = [(g, h) for g in seqs for h in range(GDN_HEADS)]
    idx = range(len(items))
    sls = [slice(h * GDN_D, (h + 1) * GDN_D) for _, h in items]
    qs = [conv[g][0][:, sls[i]] for i, (g, h) in enumerate(items)]
    ks = [conv[g][1][:, sls[i]] for i, (g, h) in enumerate(items)]
    vs = [conv[g][2][:, sls[i]] for i, (g, h) in enumerate(items)]
    qs = [q * lax.rsqrt(jnp.sum(q * q, axis=-1, keepdims=True) + L2_EPS) * (GDN_D ** -0.5) for q in qs]
    ks = [k * lax.rsqrt(jnp.sum(k * k, axis=-1, keepdims=True) + L2_EPS) for k in ks]
    betas = [beta_all[g][:, h:h + 1] for g, h in items]
    gam_cs = [gam_all[g][:, 8 + h:9 + h] for g, h in items]
    dmats = [jnp.exp(jnp.where(incl, gam_cs[i] - gam_rows[g][h:h + 1, :], -jnp.inf)) for i, (g, h) in enumerate(items)]
    kbs = [ks[i] * betas[i] for i in idx]
    egams = [jnp.exp(x) for x in gam_cs]
    a_list = [jnp.where(strict, _dot_nt(kbs[i], ks[i]) * dmats[i], 0.0) for i in idx]
    rhs = [jnp.concatenate([vs[i] * betas[i], kbs[i] * egams[i]], axis=1) for i in idx]
    sols = _solve_unit_lower(a_list, rhs)
    aqks = [_dot_nt(qs[i], ks[i]) * dmats[i] for i in idx]
    glasts = [x[BLK - 1:BLK, :] for x in gam_cs]
    ktails = [ks[i] * jnp.exp(glasts[i] - gam_cs[i]) for i in idx]
    states = [s_ref[i] for i in idx]
    wss = [_dot(jnp.concatenate([sols[i][:, GDN_D:], qs[i] * egams[i]], axis=0), states[i]) for i in idx]
    vnews = [sols[i][:, :GDN_D] - wss[i][:BLK] for i in idx]
    outs = [wss[i][BLK:] + _dot(aqks[i], vnews[i]) for i in idx]
    for i in idx:
        s_ref[i] = states[i] * jnp.exp(glasts[i]) + _dot_tn(ktails[i], vnews[i])
    for i, (g, h) in enumerate(items):
        gt = gate_ref[g, :, sls[i]].astype(F32)
        o_ref[g, :, sls[i]] = (_rmsnorm(outs[i], nw) * _silu(gt)).astype(o_ref.dtype)


def _gdn(u, small, small_t, bsz, nb, conv_w, a_log, dt_bias, norm_w):
    t = u.shape[0]
    n = GDN_HEADS * GDN_D
    grp = GDN_GROUP
    assert bsz % grp == 0
    u3 = u.reshape(bsz, nb * BLK, U_WIDTH)
    small3 = small.reshape(bsz, nb * BLK, SMALL_W)

    def lane_vec(v, off):
        return jnp.zeros((1, SMALL_W), F32).at[0, off:off + v.shape[0]].set(v.astype(F32))

    def ublk(col):
        return pl.BlockSpec((grp, BLK, n), lambda b, c: (b, c, col // n))

    def full(shape):
        return pl.BlockSpec(shape, lambda b, c: (0,) * len(shape))

    def smallt_spec(g):
        return pl.BlockSpec((SMALL_ROWS, BLK), lambda b, c: (0, (b * grp + g) * nb + c))

    y = pl.pallas_call(
        _gdn_kernel,
        grid=(bsz // grp, nb),
        in_specs=[
            ublk(COL_GDN_Q), ublk(COL_GDN_K), ublk(COL_GDN_V), ublk(COL_GDN_GATE),
            pl.BlockSpec((grp, BLK, SMALL_W), lambda b, c: (b, c, 0)),
            *[smallt_spec(g) for g in range(grp)],
            full((3 * BLK, HALO + BLK)), full((4, 3 * n)), full((1, SMALL_W)), full((1, SMALL_W)),
            full((GDN_HEADS, 1)), full((GDN_HEADS, 1)), full((1, GDN_D)),
        ],
        out_specs=pl.BlockSpec((grp, BLK, n), lambda b, c: (b, c, 0)),
        out_shape=jax.ShapeDtypeStruct((bsz, nb * BLK, n), BF16),
        scratch_shapes=[pltpu.VMEM((grp, HALO, 3 * n), U_DTYPE), pltpu.VMEM((grp * GDN_HEADS, GDN_D, GDN_D), F32)],
        compiler_params=_cparams(("parallel", "arbitrary")),
        name="gdn_mixer",
    )(u3, u3, u3, u3, small3, *([small_t] * grp), _shift_select(U_DTYPE), conv_w.astype(F32), lane_vec(a_log, 8),
      lane_vec(dt_bias, 8), a_log.astype(F32).reshape(GDN_HEADS, 1), dt_bias.astype(F32).reshape(GDN_HEADS, 1),
      norm_w.astype(F32).reshape(1, GDN_D))
    return y.reshape(t, n)


def _ssd_kernel(z_ref, x_ref, bc_ref, sm_ref, *rest):
    smt_refs = rest[:SSD_GROUP]
    (sel_ref, cw_ref, cb_ref, alog_l_ref, dtb_l_ref, alog_c_ref, dtb_c_ref, d_ref, nw_ref, e_ref, o_ref, halo_ref,
     st_ref) = rest[SSD_GROUP:]
    c = pl.program_id(1)

    @pl.when(c == 0)
    def _():
        halo_ref[...] = jnp.zeros_like(halo_ref)
        st_ref[...] = jnp.zeros_like(st_ref)

    n = SSD_HEADS * SSD_P
    gn = SSD_GROUPS * SSD_N
    gw = n // SSD_GROUPS
    hpg = SSD_HEADS // SSD_GROUPS
    cw = cw_ref[...]
    cb = cb_ref[...]
    sel = sel_ref[...]
    e = e_ref[...]
    ri = lax.broadcasted_iota(jnp.int32, (BLK, BLK), 0)
    ci = lax.broadcasted_iota(jnp.int32, (BLK, BLK), 1)
    incl = ri >= ci
    valid_rows = (c * BLK + ri) >= PAD
    ci16 = lax.broadcasted_iota(jnp.int32, (SSD_HEADS, BLK), 1)
    valid_out = (c * BLK + lax.broadcasted_iota(jnp.int32, (BLK, gw), 0)) >= PAD
    seqs = range(SSD_GROUP)

    raw = [(x_ref[s], bc_ref[s]) for s in seqs]
    halos = [halo_ref[s] for s in seqs]
    xs = [_silu(_causal_conv4(raw[s][0], halos[s][:, 0:n], cw[:, 0:n], sel) + cb[:, 0:n]) for s in seqs]
    bcs = [_silu(_causal_conv4(raw[s][1], halos[s][:, n:2 * n], cw[:, n:2 * n], sel) + cb[:, n:2 * n]) for s in seqs]
    for s in seqs:
        halo_ref[s, :, 0:n] = raw[s][0][BLK - HALO:BLK]
        halo_ref[s, :, n:2 * n] = raw[s][1][BLK - HALO:BLK]

    dtp_all = [jnp.where(valid_rows, _softplus(sm_ref[s] + dtb_l_ref[...]), 0.0) for s in seqs]
    acum_all = [_dot_sel_lhs(incl, x * (-jnp.exp(alog_l_ref[...]))) for x in dtp_all]
    dtp_x = [_dot_sel_rhs(x, e) for x in dtp_all]
    acum_x = [_dot_sel_rhs(x, e) for x in acum_all]
    dtp_rows = [jnp.where((c * BLK + ci16) >= PAD, _softplus(smt_refs[s][16:32, :] + dtb_c_ref[...]), 0.0) for s in seqs]
    acum_rows = [_dot_sel_rhs(x * (-jnp.exp(alog_c_ref[...])), ri <= ci) for x in dtp_rows]

    xdt = [xs[s] * dtp_x[s] for s in seqs]
    eacum_x = [jnp.exp(x) for x in acum_x]
    alast_x = [x[BLK - 1:BLK, :] for x in acum_x]
    xdt_dec = [xdt[s] * jnp.exp(alast_x[s] - acum_x[s]) for s in seqs]

    sg = [(s, g) for s in seqs for g in range(SSD_GROUPS)]
    sh = [(s, hh) for s in seqs for hh in range(SSD_HEADS)]
    gsl = [slice(g * gw, (g + 1) * gw) for g in range(SSD_GROUPS)]
    bgs = {(s, g): bcs[s][:, g * SSD_N:(g + 1) * SSD_N] for s, g in sg}
    cgs = {(s, g): bcs[s][:, gn + g * SSD_N:gn + (g + 1) * SSD_N] for s, g in sg}
    cbms = {k: _dot_nt(cgs[k], bgs[k]) for k in sg}
    sts = {(s, g): st_ref[s, :, gsl[g]] for s, g in sg}
    yoffs = {(s, g): _dot(cgs[s, g], sts[s, g]) * eacum_x[s][:, gsl[g]] for s, g in sg}
    lms = {(s, hh): jnp.exp(jnp.where(incl, acum_all[s][:, 16 + hh:17 + hh] - acum_rows[s][hh:hh + 1, :], -jnp.inf))
           for s, hh in sh}
    ydiag = {(s, hh): _dot(cbms[s, hh // hpg] * lms[s, hh], xdt[s][:, hh * SSD_P:(hh + 1) * SSD_P]) for s, hh in sh}
    for s, g in sg:
        st_ref[s, :, gsl[g]] = (sts[s, g] * jnp.exp(alast_x[s][:, gsl[g]])
                                + _dot_tn(bgs[s, g], xdt_dec[s][:, gsl[g]]))
    for s, g in sg:
        gs = gsl[g]
        y = yoffs[s, g] + jnp.concatenate([ydiag[s, hh] for hh in range(g * hpg, (g + 1) * hpg)], axis=1)
        y = y + d_ref[:, gs] * xs[s][:, gs]
        y = y * _silu(z_ref[s, :, gs].astype(F32))
        y = _rmsnorm(y, nw_ref[:, gs])
        o_ref[s, :, gs] = jnp.where(valid_out, y, 0.0).astype(o_ref.dtype)


def _ssd(u, small, small_t, bsz, nb, conv_w, conv_b, dt_bias, a_log, d_skip, norm_w):
    t = u.shape[0]
    n = SSD_HEADS * SSD_P
    grp = SSD_GROUP
    assert bsz % grp == 0
    u3 = u.reshape(bsz, nb * BLK, U_WIDTH)
    small3 = small.reshape(bsz, nb * BLK, SMALL_W)

    def lane_vec(v, off):
        return jnp.zeros((1, SMALL_W), F32).at[0, off:off + v.shape[0]].set(v.astype(F32))

    def ublk(col):
        return pl.BlockSpec((grp, BLK, n), lambda b, c: (b, c, col // n))

    def full(shape):
        return pl.BlockSpec(shape, lambda b, c: (0,) * len(shape))

    def smallt_spec(g):
        return pl.BlockSpec((SMALL_ROWS, BLK), lambda b, c: (0, (b * grp + g) * nb + c))

    expand = jnp.zeros((SMALL_W, n), F32).at[16:16 + SSD_HEADS].set(
        jnp.repeat(jnp.eye(SSD_HEADS, dtype=F32), SSD_P, axis=1))
    y = pl.pallas_call(
        _ssd_kernel,
        grid=(bsz // grp, nb),
        in_specs=[
            ublk(COL_SSD_Z), ublk(COL_SSD_X), ublk(COL_SSD_BC),
            pl.BlockSpec((grp, BLK, SMALL_W), lambda b, c: (b, c, 0)),
            *[smallt_spec(g) for g in range(grp)],
            full((3 * BLK, HALO + BLK)), full((4, 2 * n)), full((1, 2 * n)), full((1, SMALL_W)), full((1, SMALL_W)),
            full((SSD_HEADS, 1)), full((SSD_HEADS, 1)), full((1, n)), full((1, n)), full((SMALL_W, n)),
        ],
        out_specs=pl.BlockSpec((grp, BLK, n), lambda b, c: (b, c, 0)),
        out_shape=jax.ShapeDtypeStruct((bsz, nb * BLK, n), BF16),
        scratch_shapes=[pltpu.VMEM((grp, HALO, 2 * n), U_DTYPE), pltpu.VMEM((grp, SSD_N, n), F32)],
        compiler_params=_cparams(("parallel", "arbitrary")),
        name="ssd_mixer",
    )(u3, u3, u3, small3, *([small_t] * grp), _shift_select(U_DTYPE), conv_w.astype(F32),
      conv_b.astype(F32).reshape(1, 2 * n), lane_vec(a_log, 16), lane_vec(dt_bias, 16),
      a_log.astype(F32).reshape(SSD_HEADS, 1), dt_bias.astype(F32).reshape(SSD_HEADS, 1),
      jnp.repeat(d_skip.astype(F32), SSD_P).reshape(1, n), norm_w.astype(F32).reshape(1, n), expand)
    return y.reshape(t, n)


def _swa_kernel(sink_ref, q_ref, kc_ref, kp_ref, km_ref, vc_ref, vp_ref, vm_ref, o_ref):
    c = pl.program_id(1)
    ri = lax.broadcasted_iota(jnp.int32, (BLK, BLK), 0)
    ci = lax.broadcasted_iota(jnp.int32, (BLK, BLK), 1)
    cur_ok = (ci <= ri) & ((c > 0) | (ci >= PAD))
    prev_ok = ((c >= 2) & (ci > ri)) | ((c == 1) & (ci >= PAD))
    rep = SWA_Q_HEADS // SWA_KV_HEADS
    neg = -jnp.inf
    bias_pc = jnp.concatenate([jnp.where(prev_ok, 0.0, neg), jnp.where(cur_ok, 0.0, neg)], axis=1)
    bias_pc = jnp.concatenate([bias_pc] * rep, axis=0)
    lo = ci < SWA_DH
    lo_m = lax.broadcasted_iota(jnp.int32, (N_META, BLK), 1) < SWA_DH
    row4 = lax.broadcasted_iota(jnp.int32, (rep * BLK, 1), 0)

    def both_halves(ref, hk, lo_mask):
        x = ref[:, (hk // 2) * BLK:(hk // 2 + 1) * BLK].astype(F32)
        x = jnp.where(lo_mask, x, 0.0) if hk % 2 == 0 else jnp.where(lo_mask, 0.0, x)
        return x + pltpu.roll(x, SWA_DH, axis=1)

    def stacked_queries(hk):
        scale = SWA_DH ** -0.5
        qa = q_ref[:, (2 * hk) * BLK:(2 * hk + 1) * BLK].astype(F32) * scale
        qb = q_ref[:, (2 * hk + 1) * BLK:(2 * hk + 2) * BLK].astype(F32) * scale
        return jnp.concatenate([jnp.where(lo, qa, 0.0), jnp.where(lo, 0.0, qa),
                                jnp.where(lo, qb, 0.0), jnp.where(lo, 0.0, qb)], axis=0)

    kvh = range(SWA_KV_HEADS)
    qs = [stacked_queries(hk) for hk in kvh]
    k_pc = [jnp.concatenate([both_halves(kp_ref, hk, lo), both_halves(kc_ref, hk, lo)], axis=0) for hk in kvh]
    zrows = jnp.zeros((SINK_PAD, BLK), F32)
    k_m = [jnp.concatenate([both_halves(km_ref, hk, lo_m), zrows], axis=0) for hk in kvh]
    v_pc = [jnp.concatenate([both_halves(vp_ref, hk, lo), both_halves(vc_ref, hk, lo)], axis=0) for hk in kvh]
    v_m = [jnp.concatenate([both_halves(vm_ref, hk, lo_m), zrows], axis=0) for hk in kvh]
    col_m = lax.broadcasted_iota(jnp.int32, (1, N_META + SINK_PAD), 1)
    base_m = jnp.where((col_m < N_META) & (c >= 2), 0.0, neg)

    def sink_column(hk):
        col = jnp.full((rep * BLK, 1), sink_ref[hk * rep + rep - 1], F32)
        for r in range(rep - 2, -1, -1):
            col = jnp.where(row4 < (r + 1) * BLK, sink_ref[hk * rep + r], col)
        return col

    bias_m = [jnp.where(col_m == N_META, sink_column(hk), base_m) for hk in kvh]
    s_pc = [_dot_nt(qs[hk], k_pc[hk]) + bias_pc for hk in kvh]
    s_m = [_dot_nt(qs[hk], k_m[hk]) + bias_m[hk] for hk in kvh]
    ms = [jnp.maximum(jnp.max(s_pc[hk], axis=-1, keepdims=True), jnp.max(s_m[hk], axis=-1, keepdims=True))
          for hk in kvh]
    e_pc = [jnp.exp(s_pc[hk] - ms[hk]) for hk in kvh]
    e_m = [jnp.exp(s_m[hk] - ms[hk]) for hk in kvh]
    invs = [1.0 / (jnp.sum(e_pc[hk], axis=-1, keepdims=True) + jnp.sum(e_m[hk], axis=-1, keepdims=True)) for hk in kvh]
    outs = [(_dot(e_pc[hk], v_pc[hk]) + _dot(e_m[hk], v_m[hk])) * invs[hk] for hk in kvh]
    for hk in kvh:
        o = outs[hk]
        o_ref[:, (2 * hk) * BLK:(2 * hk + 1) * BLK] = jnp.where(lo, o[0:BLK], o[BLK:2 * BLK]).astype(o_ref.dtype)
        o_ref[:, (2 * hk + 1) * BLK:(2 * hk + 2) * BLK] = jnp.where(lo, o[2 * BLK:3 * BLK],
                                                                     o[3 * BLK:]).astype(o_ref.dtype)


def _swa(u, bsz, nb, sinks):
    t = u.shape[0]
    nq = SWA_Q_HEADS * SWA_DH
    nkv = SWA_KV_HEADS * SWA_DH
    meta_blk = PAD // N_META
    rows16 = (nb * BLK) // N_META

    def cur(col):
        return pl.BlockSpec((BLK, nkv), lambda b, c: (b * nb + c, col // nkv))

    def prev(col):
        return pl.BlockSpec((BLK, nkv), lambda b, c: (b * nb + jnp.maximum(c - 1, 0), col // nkv))

    def meta(col):
        return pl.BlockSpec((N_META, nkv), lambda b, c: (b * rows16 + meta_blk, col // nkv))

    return pl.pallas_call(
        _swa_kernel,
        grid=(bsz, nb),
        in_specs=[
            pl.BlockSpec(memory_space=pltpu.SMEM),
            pl.BlockSpec((BLK, nq), lambda b, c: (b * nb + c, COL_SWA_Q // nq)),
            cur(COL_SWA_K), prev(COL_SWA_K), meta(COL_SWA_K),
            cur(COL_SWA_V), prev(COL_SWA_V), meta(COL_SWA_V),
        ],
        out_specs=pl.BlockSpec((BLK, nq), lambda b, c: (b * nb + c, 0)),
        out_shape=jax.ShapeDtypeStruct((t, nq), BF16),
        compiler_params=_cparams(("parallel", "parallel")),
        name="swa_mixer",
    )(sinks.astype(F32), u, u, u, u, u, u, u)


def _merge_kernel(h_ref, yg_ref, ys_ref, yc_ref, ga_ref, gb_ref, gc_ref, wg_ref, ws_ref, wc_ref, wo_ref, o_ref):
    d = functools.partial(jnp.dot, preferred_element_type=F32)
    merged = (_sigmoid(ga_ref[...].astype(F32)) * d(yg_ref[...], wg_ref[...])
              + _sigmoid(gb_ref[...].astype(F32)) * d(ys_ref[...], ws_ref[...])
              + _sigmoid(gc_ref[...].astype(F32)) * d(yc_ref[...], wc_ref[...]))
    o_ref[...] = h_ref[...] + d(merged.astype(BF16), wo_ref[...])


def _merge(h, y_gdn, y_ssd, y_swa, u, w_gdn, w_ssd, w_swa, w_out):
    t = h.shape[0]
    tm = _pick_tile(t, (768, 512, 384, 256, 128))
    row = pl.BlockSpec((tm, D_MODEL), lambda i: (i, 0))
    wfull = pl.BlockSpec((D_MODEL, D_MODEL), lambda i: (0, 0))

    def gate(k):
        return pl.BlockSpec((tm, D_MODEL), lambda i: (i, COL_GATES // D_MODEL + k))

    return pl.pallas_call(
        _merge_kernel,
        grid=(t // tm,),
        in_specs=[row, row, row, row, gate(0), gate(1), gate(2), wfull, wfull, wfull, wfull],
        out_specs=row,
        out_shape=jax.ShapeDtypeStruct((t, D_MODEL), F32),
        compiler_params=_cparams(("parallel",)),
        name="merge",
    )(h, y_gdn, y_ssd, y_swa, u, u, u, w_gdn, w_ssd, w_swa, w_out)


def _mlp_kernel(h_ref, nw_ref, wu_ref, wd_ref, fw_ref, o_ref, hn_ref, acc_ref, *, final_norm):
    j = pl.program_id(1)

    @pl.when(j == 0)
    def _():
        hn_ref[...] = _rmsnorm(h_ref[...], nw_ref[...]).astype(BF16)
        acc_ref[...] = jnp.zeros_like(acc_ref)

    up = jnp.maximum(jnp.dot(hn_ref[...], wu_ref[...], preferred_element_type=F32), 0.0)
    acc_ref[...] += jnp.dot((up * up).astype(BF16), wd_ref[...], preferred_element_type=F32)

    @pl.when(j == pl.num_programs(1) - 1)
    def _():
        r = h_ref[...] + acc_ref[...]
        if final_norm:
            r = _rmsnorm(r, fw_ref[...])
        o_ref[...] = r


def _mlp(h, norm_w, w_up, w_down, final_w, final_norm, seq_rows=None):
    t = h.shape[0]
    tf = 1024
    if seq_rows is None:
        tm = _pick_tile(t, (1536, 1024, 768, 512, 384, 256, 128))
        n_rows = t
        h_spec = pl.BlockSpec((tm, D_MODEL), lambda i, j: (i, 0))
    else:
        bsz, lp, seq = seq_rows
        tm = _pick_tile(seq, (1024, 768, 512, 384, 256, 128))
        per_seq = seq // tm
        n_rows = bsz * seq
        h_spec = pl.BlockSpec((pl.Element(tm), pl.Element(D_MODEL)),
                              lambda i, j: (pl.multiple_of((i // per_seq) * lp + (lp - seq) + (i % per_seq) * tm, BLK), 0))
    return pl.pallas_call(
        functools.partial(_mlp_kernel, final_norm=final_norm),
        grid=(n_rows // tm, D_FF // tf),
        in_specs=[
            h_spec,
            pl.BlockSpec((1, D_MODEL), lambda i, j: (0, 0)),
            pl.BlockSpec((D_MODEL, tf), lambda i, j: (0, j)),
            pl.BlockSpec((tf, D_MODEL), lambda i, j: (j, 0)),
            pl.BlockSpec((1, D_MODEL), lambda i, j: (0, 0)),
        ],
        out_specs=pl.BlockSpec((tm, D_MODEL), lambda i, j: (i, 0)),
        out_shape=jax.ShapeDtypeStruct((n_rows, D_MODEL), F32),
        scratch_shapes=[pltpu.VMEM((tm, D_MODEL), BF16), pltpu.VMEM((tm, D_MODEL), F32)],
        compiler_params=_cparams(("parallel", "arbitrary")),
        name="mlp",
    )(h, norm_w, w_up, w_down, final_w)


def _permute_w_in(w):
    a_small = 4 * GDN_HEADS * GDN_D
    b_z = a_small + 2 * GDN_HEADS
    b_dt = b_z + 3 * SSD_HEADS * SSD_P
    c_q = b_dt + SSD_HEADS
    c_k = c_q + SWA_Q_HEADS * SWA_DH
    c_v = c_k + SWA_KV_HEADS * SWA_DH
    gates = c_v + SWA_KV_HEADS * SWA_DH
    main = jnp.concatenate([w[:, :a_small], w[:, b_z:b_dt], w[:, c_q:c_k], w[:, gates:], w[:, c_k:c_v],
                            w[:, c_v:gates]], axis=1)
    small = jnp.concatenate([w[:, a_small:b_z], w[:, b_dt:c_q],
                             jnp.zeros((w.shape[0], SMALL_W - SMALL_ROWS), w.dtype)], axis=1)
    return main.astype(BF16), small.astype(BF16)


def kernel(x, meta_tokens, norm1_w, w_in, gdn_conv_w, gdn_a_log, gdn_dt_bias, gdn_norm_w, ssd_conv_w, ssd_conv_b,
           ssd_dt_bias, ssd_a_log, ssd_d, ssd_norm_w, swa_sinks, w_proj_gdn, w_proj_ssd, w_proj_swa, w_out, norm2_w,
           w_up, w_down, final_norm_w):
    bsz, seq, dm = x.shape
    depth = w_in.shape[0]
    assert dm == D_MODEL and seq % BLK == 0
    nb = seq // BLK + 1
    lp = nb * BLK
    meta = jnp.broadcast_to(meta_tokens.astype(x.dtype)[None], (bsz, N_META, dm))
    h = jnp.concatenate([jnp.zeros((bsz, PAD, dm), x.dtype), meta, x], axis=1).reshape(bsz * lp, dm)
    final_w = final_norm_w.astype(F32).reshape(1, dm)
    for l in range(depth):
        w_main, w_small = _permute_w_in(w_in[l])
        u, small, small_t = _inproj(h, norm1_w[l].astype(F32).reshape(1, dm), w_main, w_small)
        y_gdn = _gdn(u, small, small_t, bsz, nb, gdn_conv_w[l], gdn_a_log[l], gdn_dt_bias[l], gdn_norm_w[l])
        y_ssd = _ssd(u, small, small_t, bsz, nb, ssd_conv_w[l], ssd_conv_b[l], ssd_dt_bias[l], ssd_a_log[l],
                     ssd_d[l], ssd_norm_w[l])
        y_swa = _swa(u, bsz, nb, swa_sinks[l])
        h = _merge(h, y_gdn, y_ssd, y_swa, u, w_proj_gdn[l].astype(BF16), w_proj_ssd[l].astype(BF16),
                   w_proj_swa[l].astype(BF16), w_out[l].astype(BF16))
        last = l == depth - 1
        h = _mlp(h, norm2_w[l].astype(F32).reshape(1, dm), w_up[l].astype(BF16), w_down[l].astype(BF16), final_w,
                 final_norm=last, seq_rows=(bsz, lp, seq) if last else None)
    return h.reshape(bsz, seq, dm)
```
